```python
import math
import jax, jax.numpy as jnp
from jax import lax
import numpy as np

D_MODEL = 2048
BATCH = 4
SEQ = 4096
DEPTH = 4

D_FF = 5632
EPS = 1e-6
ROPE_THETA = 10000.0
BLOCK = 128

SSM_GROUP = 16
SSM_WIDTH = D_MODEL // 4
SSM_GROUPS = SSM_WIDTH // SSM_GROUP
SSM_STATE = 64
DT_MIN = 1e-3
DT_MAX = 1e-1

SWA_HEADS = 8
SWA_KV_HEADS = 2
SWA_Q_PER_KV = SWA_HEADS // SWA_KV_HEADS
SWA_HEAD_DIM = 64
SWA_WINDOW = 128
SWA_WIDTH = SWA_HEADS * SWA_HEAD_DIM

MLA_HEADS = 8
MLA_Q_RANK = 512
MLA_KV_RANK = 256
MLA_NOPE_DIM = 128
MLA_ROPE_DIM = 64
MLA_V_DIM = 128
MLA_WIDTH = MLA_HEADS * MLA_V_DIM

D_MIX = SSM_WIDTH + SWA_WIDTH + MLA_WIDTH
ROPE_DIM = SWA_HEAD_DIM

IN_SIZES = (SSM_WIDTH,
            SWA_WIDTH,
            SWA_KV_HEADS * SWA_HEAD_DIM,
            SWA_KV_HEADS * SWA_HEAD_DIM,
            MLA_Q_RANK,
            MLA_KV_RANK,
            MLA_ROPE_DIM)
N_IN = sum(IN_SIZES)
IN_OFFSETS = tuple(int(o) for o in np.cumsum(IN_SIZES)[:-1])

kernel_name = "hymba_s5_swa_mla_macaron"


def _rmsnorm(x, gain):
    xf = x.astype(jnp.float32)
    xf = xf * lax.rsqrt(jnp.mean(xf * xf, axis=-1, keepdims=True) + EPS)
    return xf.astype(x.dtype) * gain


def _rope_tables(positions, dim):
    inv_freq = ROPE_THETA ** (-jnp.arange(0, dim, 2, dtype=jnp.float32) / dim)
    ang = positions.astype(jnp.float32)[..., None] * inv_freq
    return jnp.cos(ang), jnp.sin(ang)


def _apply_rope(x, cos, sin):
    c = cos[:, :, None, :].astype(x.dtype)
    s = sin[:, :, None, :].astype(x.dtype)
    x1, x2 = jnp.split(x, 2, axis=-1)
    return jnp.concatenate([x1 * c - x2 * s, x2 * c + x1 * s], axis=-1)


def _swiglu(h, w_gate, w_up, w_down):
    return (jax.nn.silu(h @ w_gate) * (h @ w_up)) @ w_down


def _complex_affine_combine(left, right):
    a1r, a1i, b1r, b1i = left
    a2r, a2i, b2r, b2i = right
    ar = a2r * a1r - a2i * a1i
    ai = a2r * a1i + a2i * a1r
    br = a2r * b1r - a2i * b1i + b2r
    bi = a2r * b1i + a2i * b1r + b2i
    return (ar, ai, br, bi)


def _s5_mixer(u, log_dt, a_re, a_im, b_re, b_im, c_re, c_im, d_skip, w_glu, b_glu):
    B_, L, _ = u.shape
    f32 = jnp.float32
    uf = u.astype(f32).reshape(B_, L, SSM_GROUPS, SSM_GROUP)
    lr, li = a_re.astype(f32), a_im.astype(f32)
    dt = jnp.exp(log_dt.astype(f32))[:, None]
    mag = jnp.exp(lr * dt)
    abar_r = mag * jnp.cos(li * dt)
    abar_i = mag * jnp.sin(li * dt)
    den = lr * lr + li * li
    nr = abar_r - 1.0
    qr = (nr * lr + abar_i * li) / den
    qi = (abar_i * lr - nr * li) / den
    br, bi = b_re.astype(f32), b_im.astype(f32)
    bbar_r = qr[..., None] * br - qi[..., None] * bi
    bbar_i = qr[..., None] * bi + qi[..., None] * br
    bu_r = jnp.einsum('blgc,gpc->blgp', uf, bbar_r)
    bu_i = jnp.einsum('blgc,gpc->blgp', uf, bbar_i)
    shape = bu_r.shape
    elems = (jnp.broadcast_to(abar_r, shape), jnp.broadcast_to(abar_i, shape), bu_r, bu_i)
    _, _, xr, xi = lax.associative_scan(_complex_affine_combine, elems, axis=1)
    y = (jnp.einsum('blgp,gcp->blgc', xr, c_re.astype(f32))
         - jnp.einsum('blgp,gcp->blgc', xi, c_im.astype(f32)))
    y = y.reshape(B_, L, SSM_WIDTH) + d_skip.astype(f32) * u.astype(f32)
    y = jax.nn.gelu(y).astype(u.dtype)
    return y * jax.nn.sigmoid(y @ w_glu + b_glu)


def _swa_mixer(q, k, v, cos, sin, sinks):
    B_, L = q.shape[:2]
    nb = L // BLOCK
    f32 = jnp.float32
    q = _apply_rope(q.reshape(B_, L, SWA_HEADS, SWA_HEAD_DIM), cos, sin)
    k = _apply_rope(k.reshape(B_, L, SWA_KV_HEADS, SWA_HEAD_DIM), cos, sin)
    v = v.reshape(B_, L, SWA_KV_HEADS, SWA_HEAD_DIM)
    qb = q.reshape(B_, nb, BLOCK, SWA_KV_HEADS, SWA_Q_PER_KV, SWA_HEAD_DIM)

    def with_prev(t):
        tb = t.reshape(B_, nb, BLOCK, SWA_KV_HEADS, SWA_HEAD_DIM)
        prev = jnp.concatenate([jnp.zeros_like(tb[:, :1]), tb[:, :-1]], axis=1)
        return jnp.concatenate([prev, tb], axis=2)

    kb, vb = with_prev(k), with_prev(v)
    s = jnp.einsum('bnqhgd,bnkhd->bnhgqk', qb, kb).astype(f32) * (SWA_HEAD_DIM ** -0.5)
    qi = jnp.arange(BLOCK)[:, None] + BLOCK
    kj = jnp.arange(2 * BLOCK)[None, :]
    band = (qi - kj >= 0) & (qi - kj < SWA_WINDOW)
    has_prev = (jnp.arange(nb) > 0)[:, None, None] | (kj >= BLOCK)[None]
    mask = band[None] & has_prev
    s = jnp.where(mask[None, :, None, None], s, -jnp.inf)
    sink = sinks.astype(f32).reshape(1, 1, SWA_KV_HEADS, SWA_Q_PER_KV, 1, 1)
    m = jnp.maximum(jnp.max(s, axis=-1, keepdims=True), sink)
    e = jnp.exp(s - m)
    p = e / (jnp.sum(e, axis=-1, keepdims=True) + jnp.exp(sink - m))
    o = jnp.einsum('bnhgqk,bnkhd->bnqhgd', p.astype(v.dtype), vb)
    return o.reshape(B_, L, SWA_WIDTH)


def _mla_mixer(c_q, c_kv, k_rope, cos, sin, q_norm, w_uq, kv_norm, w_ukv):
    B_, L = c_q.shape[:2]
    nb = L // BLOCK
    f32 = jnp.float32
    q = (_rmsnorm(c_q, q_norm) @ w_uq).reshape(B_, L, MLA_HEADS, MLA_NOPE_DIM + MLA_ROPE_DIM)
    q_nope = q[..., :MLA_NOPE_DIM]
    q_rope = _apply_rope(q[..., MLA_NOPE_DIM:], cos, sin)
    kv = (_rmsnorm(c_kv, kv_norm) @ w_ukv).reshape(B_, L, MLA_HEADS, MLA_NOPE_DIM + MLA_V_DIM)
    k_nope, v = kv[..., :MLA_NOPE_DIM], kv[..., MLA_NOPE_DIM:]
    k_rope = _apply_rope(k_rope[:, :, None, :], cos, sin)[:, :, 0]
    scale = (MLA_NOPE_DIM + MLA_ROPE_DIM) ** -0.5
    key_pos = jnp.arange(L)

    def to_blocks(t):
        return jnp.moveaxis(t.reshape(B_, nb, BLOCK, *t.shape[2:]), 1, 0)

    def block(args):
        qn, qr, n = args
        s = (jnp.einsum('bqhd,bkhd->bhqk', qn, k_nope)
             + jnp.einsum('bqhd,bkd->bhqk', qr, k_rope)).astype(f32) * scale
        q_pos = n * BLOCK + jnp.arange(BLOCK)
        s = jnp.where(key_pos[None, :] <= q_pos[:, None], s, -jnp.inf)
        p = jax.nn.softmax(s, axis=-1).astype(v.dtype)
        return jnp.einsum('bhqk,bkhd->bqhd', p, v)

    o = lax.map(block, (to_blocks(q_nope), to_blocks(q_rope), jnp.arange(nb)))
    return jnp.moveaxis(o, 0, 1).reshape(B_, L, MLA_WIDTH)


def setup_inputs(seed: int = 0) -> dict:
    key = jax.random.key(seed)
    ks = jax.random.split(key, 40)
    f32 = jnp.float32

    def nrm(k, shape, scale):
        return jax.random.normal(k, shape, f32) * scale

    def gain(k, shape):
        return 1.0 + 0.02 * jax.random.normal(k, shape, f32)

    x = jax.random.normal(ks[0], (BATCH, SEQ, D_MODEL), f32)
    offset = jax.random.randint(ks[1], (BATCH, 1), 0, 1024, dtype=jnp.int32)
    positions = (offset + jnp.arange(SEQ, dtype=jnp.int32)[None, :]).astype(jnp.int32)
    n_idx = jnp.arange(SSM_STATE, dtype=f32)
    return {
        "x": x,
        "positions": positions,
        "ffn1_norm": gain(ks[2], (DEPTH, D_MODEL)),
        "ffn1_w_gate": nrm(ks[3], (DEPTH, D_MODEL, D_FF), D_MODEL ** -0.5),
        "ffn1_w_up": nrm(ks[4], (DEPTH, D_MODEL, D_FF), D_MODEL ** -0.5),
        "ffn1_w_down": nrm(ks[5], (DEPTH, D_FF, D_MODEL), D_FF ** -0.5),
        "mix_norm": gain(ks[6], (DEPTH, D_MODEL)),
        "w_in": nrm(ks[7], (DEPTH, D_MODEL, N_IN), D_MODEL ** -0.5),
        "ssm_log_dt": jax.random.uniform(ks[8], (DEPTH, SSM_GROUPS), f32, math.log(DT_MIN), math.log(DT_MAX)),
        "ssm_a_re": -0.5 * jnp.exp(0.05 * jax.random.normal(ks[9], (DEPTH, SSM_GROUPS, SSM_STATE), f32)),
        "ssm_a_im": math.pi * n_idx + 0.01 * jax.random.normal(ks[10], (DEPTH, SSM_GROUPS, SSM_STATE), f32),
        "ssm_b_re": nrm(ks[11], (DEPTH, SSM_GROUPS, SSM_STATE, SSM_GROUP), (2 * SSM_GROUP) ** -0.5),
        "ssm_b_im": nrm(ks[12], (DEPTH, SSM_GROUPS, SSM_STATE, SSM_GROUP), (2 * SSM_GROUP) ** -0.5),
        "ssm_c_re": nrm(ks[13], (DEPTH, SSM_GROUPS, SSM_GROUP, SSM_STATE), (2 * SSM_STATE) ** -0.5),
        "ssm_c_im": nrm(ks[14], (DEPTH, SSM_GROUPS, SSM_GROUP, SSM_STATE), (2 * SSM_STATE) ** -0.5),
        "ssm_d": nrm(ks[15], (DEPTH, SSM_WIDTH), 0.5),
        "ssm_w_glu": nrm(ks[16], (DEPTH, SSM_WIDTH, SSM_WIDTH), SSM_WIDTH ** -0.5),
        "ssm_b_glu": nrm(ks[17], (DEPTH, SSM_WIDTH), 0.01),
        "swa_sinks": nrm(ks[18], (DEPTH, SWA_HEADS), 0.5),
        "mla_q_norm": gain(ks[19], (DEPTH, MLA_Q_RANK)),
        "mla_w_uq": nrm(ks[20], (DEPTH, MLA_Q_RANK, MLA_HEADS * (MLA_NOPE_DIM + MLA_ROPE_DIM)), MLA_Q_RANK ** -0.5),
        "mla_kv_norm": gain(ks[21], (DEPTH, MLA_KV_RANK)),
        "mla_w_ukv": nrm(ks[22], (DEPTH, MLA_KV_RANK, MLA_HEADS * (MLA_NOPE_DIM + MLA_V_DIM)), MLA_KV_RANK ** -0.5),
        "out_norm": gain(ks[23], (DEPTH, D_MIX)),
        "w_out": nrm(ks[24], (DEPTH, D_MIX, D_MODEL), D_MIX ** -0.5),
        "ffn2_norm": gain(ks[25], (DEPTH, D_MODEL)),
        "ffn2_w_gate": nrm(ks[26], (DEPTH, D_MODEL, D_FF), D_MODEL ** -0.5),
        "ffn2_w_up": nrm(ks[27], (DEPTH, D_MODEL, D_FF), D_MODEL ** -0.5),
        "ffn2_w_down": nrm(ks[28], (DEPTH, D_FF, D_MODEL), D_FF ** -0.5),
        "final_norm": gain(ks[29], (D_MODEL,)),
    }


def reference(x, positions, ffn1_norm, ffn1_w_gate, ffn1_w_up, ffn1_w_down, mix_norm, w_in,
              ssm_log_dt, ssm_a_re, ssm_a_im, ssm_b_re, ssm_b_im, ssm_c_re, ssm_c_im, ssm_d,
              ssm_w_glu, ssm_b_glu, swa_sinks, mla_q_norm, mla_w_uq, mla_kv_norm, mla_w_ukv,
              out_norm, w_out, ffn2_norm, ffn2_w_gate, ffn2_w_up, ffn2_w_down, final_norm):
    cos, sin = _rope_tables(positions, ROPE_DIM)
    s1 = SSM_WIDTH
    s2 = SSM_WIDTH + SWA_WIDTH
    for l in range(DEPTH):
        h = _rmsnorm(x, ffn1_norm[l])
        x = x + 0.5 * _swiglu(h, ffn1_w_gate[l], ffn1_w_up[l], ffn1_w_down[l])
        h = _rmsnorm(x, mix_norm[l])
        z = h @ w_in[l]
        u, q_s, k_s, v_s, c_q, c_kv, k_r = jnp.split(z, IN_OFFSETS, axis=-1)
        y_ssm = _s5_mixer(u, ssm_log_dt[l], ssm_a_re[l], ssm_a_im[l], ssm_b_re[l], ssm_b_im[l],
                          ssm_c_re[l], ssm_c_im[l], ssm_d[l], ssm_w_glu[l], ssm_b_glu[l])
        y_swa = _swa_mixer(q_s, k_s, v_s, cos, sin, swa_sinks[l])
        y_mla = _mla_mixer(c_q, c_kv, k_r, cos, sin, mla_q_norm[l], mla_w_uq[l],
                           mla_kv_norm[l], mla_w_ukv[l])
        y = jnp.concatenate([_rmsnorm(y_ssm, out_norm[l, :s1]),
                             _rmsnorm(y_swa, out_norm[l, s1:s2]),
                             _rmsnorm(y_mla, out_norm[l, s2:])], axis=-1)
        x = x + y @ w_out[l]
        h = _rmsnorm(x, ffn2_norm[l])
        x = x + 0.5 * _swiglu(h, ffn2_w_gate[l], ffn2_w_up[l], ffn2_w_down[l])
    return _rmsnorm(x, final_norm)
```

```python
import functools
import math

import jax
import jax.numpy as jnp
import numpy as np
from jax import lax
from jax.experimental import pallas as pl
from jax.experimental.pallas import tpu as pltpu

F32 = jnp.float32
BF16 = jnp.bfloat16

D_MODEL = 2048
D_FF = 5632
EPS = 1e-6
ROPE_THETA = 10000.0

SSM_GROUP = 16
SSM_WIDTH = 512
SSM_GROUPS = 32
SSM_STATE = 64
SSM_HALF_GROUPS = 16
SSM_HALF_STATES = SSM_HALF_GROUPS * SSM_STATE
SSM_STREAMS = 8

SWA_HEADS = 8
SWA_Q_PER_KV = 4
SWA_HEAD_DIM = 64
SWA_WINDOW = 128
SWA_WIDTH = 512
SWA_KV_WIDTH = 128

MLA_HEADS = 8
MLA_Q_RANK = 512
MLA_KV_RANK = 256
MLA_NOPE = 128
MLA_ROPE = 64
MLA_V = 128
MLA_WIDTH = 1024
MLA_QK_PAD = 256
ROPE_DIM = 64
LANES = 128

VMEM_LIMIT = 56 * 1024 * 1024

FFN_TM = 512
FFN_TF = 512
MIX_TM = 512
SSM_T = 256
SWA_TQ = 256
MLA_TQ = 512
OUT_TM = 512
ROPE_TM = 2048

_C_U = 0
_C_QS = 512
_C_QS_ROT = 1024
_C_KS = 1536
_C_KS_ROT = 1664
_C_VS = 1792
_C_CQ = 1920
_C_CKV = 2432
_C_KR = 2688
_C_KR_ROT = 2816
N_EXT = 2944


def _rms(x, gain):
    ms = jnp.mean(x * x, axis=-1, keepdims=True)
    return x * lax.rsqrt(ms + EPS) * gain


def _sigmoid(x):
    return 1.0 / (1.0 + jnp.exp(-x))


def _const_spec(shape):
    nd = len(shape)
    return pl.BlockSpec(shape, lambda *_: (0,) * nd, pipeline_mode=pl.Buffered(1))


def _rope_kernel(pos_ref, freq_ref, sign_ref, cos_ref, sin_ref):
    ang = pos_ref[...].astype(F32) * freq_ref[...]
    cos_ref[...] = jnp.cos(ang)
    sin_ref[...] = jnp.sin(ang) * sign_ref[...]


def _rope_tables(positions):
    m = positions.size
    tm = min(ROPE_TM, m)
    inv_freq = ROPE_THETA ** (-jnp.arange(0, ROPE_DIM, 2, dtype=F32) / ROPE_DIM)
    freq4 = jnp.tile(inv_freq, 4)[None, :]
    sign4 = jnp.tile(jnp.concatenate([-jnp.ones(32, F32), jnp.ones(32, F32)]), 2)[None, :]
    return pl.pallas_call(
        _rope_kernel,
        grid=(m // tm,),
        in_specs=[pl.BlockSpec((tm, 1), lambda i: (i, 0)),
                  pl.BlockSpec((1, LANES), lambda i: (0, 0)),
                  pl.BlockSpec((1, LANES), lambda i: (0, 0))],
        out_specs=[pl.BlockSpec((tm, LANES), lambda i: (i, 0))] * 2,
        out_shape=[jax.ShapeDtypeStruct((m, LANES), F32)] * 2,
        name="rope_tables",
    )(positions.reshape(m, 1), freq4, sign4)


def _ffn_kernel(x_ref, g_ref, wg_ref, wu_ref, wd_ref, o_ref, h_ref):
    f = pl.program_id(1)

    @pl.when(f == 0)
    def _():
        x = x_ref[...]
        h_ref[...] = _rms(x, g_ref[...]).astype(BF16)
        o_ref[...] = x

    h = h_ref[...]
    g = jnp.dot(h, wg_ref[...], preferred_element_type=F32)
    u = jnp.dot(h, wu_ref[...], preferred_element_type=F32)
    a = (g * _sigmoid(g) * u).astype(BF16)
    o_ref[...] += jnp.dot(a, wd_ref[...], preferred_element_type=F32)


def _ffn(x, gain, wg, wu, wd_half):
    m = x.shape[0]
    tm = min(FFN_TM, m)
    return pl.pallas_call(
        _ffn_kernel,
        grid=(m // tm, D_FF // FFN_TF),
        in_specs=[pl.BlockSpec((tm, D_MODEL), lambda i, f: (i, 0)),
                  pl.BlockSpec((1, D_MODEL), lambda i, f: (0, 0)),
                  pl.BlockSpec((D_MODEL, FFN_TF), lambda i, f: (0, f)),
                  pl.BlockSpec((D_MODEL, FFN_TF), lambda i, f: (0, f)),
                  pl.BlockSpec((FFN_TF, D_MODEL), lambda i, f: (f, 0))],
        out_specs=pl.BlockSpec((tm, D_MODEL), lambda i, f: (i, 0)),
        out_shape=jax.ShapeDtypeStruct((m, D_MODEL), F32),
        scratch_shapes=[pltpu.VMEM((tm, D_MODEL), BF16)],
        compiler_params=pltpu.CompilerParams(
            dimension_semantics=("parallel", "arbitrary"), vmem_limit_bytes=VMEM_LIMIT),
        name="ffn",
    )(x, gain, wg, wu, wd_half)


def _mix_in_kernel(x_ref, g_ref, w_ref, cos_ref, sin_ref, qn_ref, kvn_ref, wa_ref, wb_ref, wkv_ref,
                   u_ref, qs_ref, ks_ref, vs_ref, qc_ref, kc_ref, v_ref):
    h = _rms(x_ref[...], g_ref[...]).astype(BF16)

    def seg(lo, n):
        return jnp.dot(h, w_ref[:, lo:lo + n], preferred_element_type=F32)

    cos4 = cos_ref[...]
    sin4 = sin_ref[...]
    cos512 = jnp.concatenate([cos4] * 4, axis=1)
    sin512 = jnp.concatenate([sin4] * 4, axis=1)

    u_ref[...] = seg(_C_U, SSM_WIDTH)
    qs = seg(_C_QS, SWA_WIDTH) * cos512 + seg(_C_QS_ROT, SWA_WIDTH) * sin512
    qs_ref[...] = (qs * (SWA_HEAD_DIM ** -0.5)).astype(BF16)
    ks_ref[...] = (seg(_C_KS, SWA_KV_WIDTH) * cos4 + seg(_C_KS_ROT, SWA_KV_WIDTH) * sin4).astype(BF16)
    vs_ref[...] = seg(_C_VS, SWA_KV_WIDTH).astype(BF16)

    cqn = _rms(seg(_C_CQ, MLA_Q_RANK), qn_ref[...]).astype(BF16)
    ckvn = _rms(seg(_C_CKV, MLA_KV_RANK), kvn_ref[...]).astype(BF16)
    kr = (seg(_C_KR, LANES) * cos4 + seg(_C_KR_ROT, LANES) * sin4).astype(BF16)

    scale = (MLA_NOPE + MLA_ROPE) ** -0.5
    for hd in range(MLA_HEADS):
        qa = jnp.dot(cqn, wa_ref[:, hd * 256:(hd + 1) * 256], preferred_element_type=F32)
        qb = jnp.dot(cqn, wb_ref[:, hd * LANES:(hd + 1) * LANES], preferred_element_type=F32)
        qc_ref[hd, :, 0:LANES] = (qa[:, :LANES] * scale).astype(BF16)
        qc_ref[hd, :, LANES:2 * LANES] = ((qa[:, LANES:] * cos4 + qb * sin4) * scale).astype(BF16)
        kv = jnp.dot(ckvn, wkv_ref[:, hd * 256:(hd + 1) * 256], preferred_element_type=F32)
        kc_ref[hd, :, 0:LANES] = kv[:, :LANES].astype(BF16)
        kc_ref[hd, :, LANES:2 * LANES] = kr
        v_ref[hd] = kv[:, LANES:].astype(BF16)


def _mix_in(x, gain, w_ext, cos4, sin4, qn, kvn, wa, wb, wkv, batch, seq):
    m = x.shape[0]
    tm = min(MIX_TM, seq)
    nt = seq // tm
    row = lambda b, i: (b * nt + i, 0)
    head_spec = lambda w: pl.BlockSpec((None, MLA_HEADS, tm, w), lambda b, i: (b, 0, i, 0))
    return pl.pallas_call(
        _mix_in_kernel,
        grid=(batch, nt),
        in_specs=[pl.BlockSpec((tm, D_MODEL), row),
                  _const_spec((1, D_MODEL)),
                  _const_spec((D_MODEL, N_EXT)),
                  pl.BlockSpec((tm, LANES), row),
                  pl.BlockSpec((tm, LANES), row),
                  _const_spec((1, MLA_Q_RANK)),
                  _const_spec((1, MLA_KV_RANK)),
                  _const_spec((MLA_Q_RANK, MLA_HEADS * 256)),
                  _const_spec((MLA_Q_RANK, MLA_HEADS * LANES)),
                  _const_spec((MLA_KV_RANK, MLA_HEADS * 256))],
        out_specs=[pl.BlockSpec((tm, SSM_WIDTH), row),
                   pl.BlockSpec((tm, SWA_WIDTH), row),
                   pl.BlockSpec((tm, SWA_KV_WIDTH), row),
                   pl.BlockSpec((tm, SWA_KV_WIDTH), row),
                   head_spec(MLA_QK_PAD), head_spec(MLA_QK_PAD), head_spec(MLA_V)],
        out_shape=[jax.ShapeDtypeStruct((m, SSM_WIDTH), F32),
                   jax.ShapeDtypeStruct((m, SWA_WIDTH), BF16),
                   jax.ShapeDtypeStruct((m, SWA_KV_WIDTH), BF16),
                   jax.ShapeDtypeStruct((m, SWA_KV_WIDTH), BF16),
                   jax.ShapeDtypeStruct((batch, MLA_HEADS, seq, MLA_QK_PAD), BF16),
                   jax.ShapeDtypeStruct((batch, MLA_HEADS, seq, MLA_QK_PAD), BF16),
                   jax.ShapeDtypeStruct((batch, MLA_HEADS, seq, MLA_V), BF16)],
        compiler_params=pltpu.CompilerParams(
            dimension_semantics=("parallel", "parallel"), vmem_limit_bytes=VMEM_LIMIT),
        name="mix_in",
    )(x, gain, w_ext, cos4, sin4, qn, kvn, wa, wb, wkv)


def _ssm_kernel(u_ref, p_ref, pt_ref, wb_ref, wc_ref, lam_ref, lamn_ref, d_ref, wglu_ref, bglu_ref,
                y_ref, s_ref, carry_ref, f_ref, *, t, lane_blk):
    n = t // SSM_STREAMS
    hs = SSM_HALF_STATES

    @pl.when(pl.program_id(1) == 0)
    def _():
        carry_ref[...] = jnp.zeros_like(carry_ref)

    u = u_ref[...]
    up = jnp.dot(p_ref[...], u.astype(BF16), preferred_element_type=F32).astype(BF16)
    for hf in range(2):
        s_ref[:, hf * 2 * hs:(hf + 1) * 2 * hs] = jnp.dot(
            up[:, hf * 256:(hf + 1) * 256], wb_ref[hf], preferred_element_type=F32)

    row_id = lax.broadcasted_iota(jnp.int32, (SSM_STREAMS, lane_blk), 0)
    for hf in range(2):
        for lb in range(hs // lane_blk):
            re0 = hf * 2 * hs + lb * lane_blk
            im0 = re0 + hs
            re_sl = slice(re0, re0 + lane_blk)
            im_sl = slice(im0, im0 + lane_blk)
            ar = jnp.broadcast_to(lam_ref[:, re_sl], (SSM_STREAMS, lane_blk))
            ai = jnp.broadcast_to(lam_ref[:, im_sl], (SSM_STREAMS, lane_blk))

            def scan(x0r, x0i, store):
                def body(i, c):
                    xr, xi = c
                    r0 = pl.multiple_of(i * SSM_STREAMS, SSM_STREAMS)
                    br = s_ref[pl.ds(r0, SSM_STREAMS), re_sl]
                    bi = s_ref[pl.ds(r0, SSM_STREAMS), im_sl]
                    nr = ar * xr - ai * xi + br
                    ni = ar * xi + ai * xr + bi
                    if store:
                        s_ref[pl.ds(r0, SSM_STREAMS), re_sl] = nr
                        s_ref[pl.ds(r0, SSM_STREAMS), im_sl] = ni
                    return nr, ni
                return lax.fori_loop(0, n, body, (x0r, x0i))

            zero = jnp.zeros((SSM_STREAMS, lane_blk), F32)
            fr, fi = scan(zero, zero, False)
            f_ref[:, 0:lane_blk] = fr
            f_ref[:, lane_blk:2 * lane_blk] = fi
            pr = lamn_ref[:, re_sl]
            pi = lamn_ref[:, im_sl]
            sr = carry_ref[:, re_sl]
            si = carry_ref[:, im_sl]
            s0r = jnp.zeros((SSM_STREAMS, lane_blk), F32)
            s0i = jnp.zeros((SSM_STREAMS, lane_blk), F32)
            for k in range(SSM_STREAMS):
                s0r = jnp.where(row_id == k, jnp.broadcast_to(sr, s0r.shape), s0r)
                s0i = jnp.where(row_id == k, jnp.broadcast_to(si, s0i.shape), s0i)
                er = f_ref[k:k + 1, 0:lane_blk]
                ei = f_ref[k:k + 1, lane_blk:2 * lane_blk]
                sr, si = pr * sr - pi * si + er, pr * si + pi * sr + ei
            carry_ref[:, re_sl] = sr
            carry_ref[:, im_sl] = si
            scan(s0r, s0i, True)

    yp = []
    for hf in range(2):
        xb = s_ref[:, hf * 2 * hs:(hf + 1) * 2 * hs].astype(BF16)
        yp.append(jnp.dot(xb, wc_ref[hf], preferred_element_type=F32))
    yp = jnp.concatenate(yp, axis=1)
    y_hi = yp.astype(BF16)
    y_lo = (yp - y_hi.astype(F32)).astype(BF16)
    pt = pt_ref[...]
    y = (jnp.dot(pt, y_hi, preferred_element_type=F32)
         + jnp.dot(pt, y_lo, preferred_element_type=F32))
    y = y + d_ref[...] * u
    c0 = math.sqrt(2.0 / math.pi)
    y = 0.5 * y * (1.0 + jnp.tanh(c0 * (y + 0.044715 * (y * y * y))))
    z = jnp.dot(y.astype(BF16), wglu_ref[...], preferred_element_type=F32) + bglu_ref[...]
    y_ref[...] = y * _sigmoid(z)


def _ssm(u, perm, perm_t, wb, wc, lam, lamn, d_skip, wglu, bglu, batch, seq):
    m = u.shape[0]
    t = min(SSM_T, seq)
    nt = seq // t
    row = lambda b, i: (b * nt + i, 0)
    kern = functools.partial(_ssm_kernel, t=t, lane_blk=512)
    return pl.pallas_call(
        kern,
        grid=(batch, nt),
        in_specs=[pl.BlockSpec((t, SSM_WIDTH), row),
                  _const_spec((t, t)), _const_spec((t, t)),
                  _const_spec((2, 256, 2 * SSM_HALF_STATES)),
                  _const_spec((2, 2 * SSM_HALF_STATES, 256)),
                  _const_spec((1, 4 * SSM_HALF_STATES)),
                  _const_spec((1, 4 * SSM_HALF_STATES)),
                  _const_spec((1, SSM_WIDTH)),
                  _const_spec((SSM_WIDTH, SSM_WIDTH)),
                  _const_spec((1, SSM_WIDTH))],
        out_specs=pl.BlockSpec((t, SSM_WIDTH), row),
        out_shape=jax.ShapeDtypeStruct((m, SSM_WIDTH), F32),
        scratch_shapes=[pltpu.VMEM((t, 4 * SSM_HALF_STATES), F32),
                        pltpu.VMEM((1, 4 * SSM_HALF_STATES), F32),
                        pltpu.VMEM((SSM_STREAMS, 1024), F32)],
        compiler_params=pltpu.CompilerParams(
            dimension_semantics=("parallel", "arbitrary"), vmem_limit_bytes=VMEM_LIMIT),
        name="ssm",
    )(u, perm, perm_t, wb, wc, lam, lamn, d_skip, wglu, bglu)


def _swa_kernel(sink_ref, q_ref, k_ref, v_ref, o_ref, *, tq):
    i = pl.program_id(1)
    tk = tq + SWA_WINDOW
    start = pl.multiple_of(jnp.maximum(i * tq - SWA_WINDOW, 0), SWA_WINDOW)
    kk = k_ref[pl.ds(start, tk), :]
    vv = v_ref[pl.ds(start, tk), :]
    qpos = i * tq + lax.broadcasted_iota(jnp.int32, (tq, tk), 0)
    kpos = start + lax.broadcasted_iota(jnp.int32, (tq, tk), 1)
    dist = qpos - kpos
    mask = (dist >= 0) & (dist < SWA_WINDOW)
    outs = []
    for hd in range(SWA_HEADS):
        g = hd // SWA_Q_PER_KV
        qh = q_ref[:, hd * SWA_HEAD_DIM:(hd + 1) * SWA_HEAD_DIM]
        kh = kk[:, g * SWA_HEAD_DIM:(g + 1) * SWA_HEAD_DIM]
        vh = vv[:, g * SWA_HEAD_DIM:(g + 1) * SWA_HEAD_DIM]
        s = lax.dot_general(qh, kh, (((1,), (1,)), ((), ())), preferred_element_type=F32)
        s = jnp.where(mask, s, -jnp.inf)
        sink = sink_ref[hd]
        mx = jnp.maximum(jnp.max(s, axis=-1, keepdims=True), sink)
        e = jnp.exp(s - mx)
        den = jnp.sum(e, axis=-1, keepdims=True) + jnp.exp(sink - mx)
        p = (e / den).astype(BF16)
        outs.append(jnp.dot(p, vh, preferred_element_type=F32))
    o_ref[...] = jnp.concatenate(outs, axis=1)


def _swa(sinks, q, k, v, batch, seq):
    m = q.shape[0]
    tq = min(SWA_TQ, seq)
    nt = seq // tq
    kern = functools.partial(_swa_kernel, tq=tq)
    k3 = k.reshape(batch, seq, SWA_KV_WIDTH)
    v3 = v.reshape(batch, seq, SWA_KV_WIDTH)
    return pl.pallas_call(
        kern,
        grid=(batch, nt),
        in_specs=[pl.BlockSpec(memory_space=pltpu.SMEM),
                  pl.BlockSpec((tq, SWA_WIDTH), lambda b, i: (b * nt + i, 0)),
                  pl.BlockSpec((None, seq, SWA_KV_WIDTH), lambda b, i: (b, 0, 0)),
                  pl.BlockSpec((None, seq, SWA_KV_WIDTH), lambda b, i: (b, 0, 0))],
        out_specs=pl.BlockSpec((tq, SWA_WIDTH), lambda b, i: (b * nt + i, 0)),
        out_shape=jax.ShapeDtypeStruct((m, SWA_WIDTH), F32),
        compiler_params=pltpu.CompilerParams(
            dimension_semantics=("parallel", "parallel"), vmem_limit_bytes=VMEM_LIMIT),
        name="swa",
    )(sinks, q, k3, v3)


def _mla_kernel(q_ref, k_ref, v_ref, o_ref, m_ref, l_ref, acc_ref, *, tq):
    qi = pl.program_id(2)
    q = q_ref[...]
    m_ref[...] = jnp.full_like(m_ref, -jnp.inf)
    l_ref[...] = jnp.zeros_like(l_ref)
    acc_ref[...] = jnp.zeros_like(acc_ref)

    def block(j, masked):
        r0 = pl.multiple_of(j * tq, tq)
        k = k_ref[pl.ds(r0, tq), :]
        v = v_ref[pl.ds(r0, tq), :]
        s = lax.dot_general(q, k, (((1,), (1,)), ((), ())), preferred_element_type=F32)
        if masked:
            row = lax.broadcasted_iota(jnp.int32, (tq, tq), 0)
            col = lax.broadcasted_iota(jnp.int32, (tq, tq), 1)
            s = jnp.where(col <= row, s, -jnp.inf)
        m_old = m_ref[...]
        m_new = jnp.maximum(m_old, jnp.max(s, axis=-1, keepdims=True))
        alpha = jnp.exp(m_old - m_new)
        p = jnp.exp(s - m_new)
        l_ref[...] = alpha * l_ref[...] + jnp.sum(p, axis=-1, keepdims=True)
        acc_ref[...] = alpha * acc_ref[...] + jnp.dot(p.astype(BF16), v, preferred_element_type=F32)
        m_ref[...] = m_new

    def body(j, c):
        block(j, False)
        return c

    lax.fori_loop(0, qi, body, 0)
    block(qi, True)
    o_ref[...] = acc_ref[...] / l_ref[...]


def _mla(qc, kc, v, batch, seq):
    tq = min(MLA_TQ, seq)
    nt = seq // tq
    kern = functools.partial(_mla_kernel, tq=tq)
    return pl.pallas_call(
        kern,
        grid=(batch, MLA_HEADS, nt),
        in_specs=[pl.BlockSpec((None, None, tq, MLA_QK_PAD), lambda b, h, i: (b, h, i, 0)),
                  pl.BlockSpec((None, None, seq, MLA_QK_PAD), lambda b, h, i: (b, h, 0, 0)),
                  pl.BlockSpec((None, None, seq, MLA_V), lambda b, h, i: (b, h, 0, 0))],
        out_specs=pl.BlockSpec((tq, MLA_V), lambda b, h, i: (b * nt + i, h)),
        out_shape=jax.ShapeDtypeStruct((batch * seq, MLA_WIDTH), F32),
        scratch_shapes=[pltpu.VMEM((tq, 1), F32), pltpu.VMEM((tq, 1), F32),
                        pltpu.VMEM((tq, MLA_V), F32)],
        compiler_params=pltpu.CompilerParams(
            dimension_semantics=("parallel", "parallel", "parallel"), vmem_limit_bytes=VMEM_LIMIT),
        name="mla",
    )(qc, kc, v)


def _out_kernel(x_ref, ys_ref, yw_ref, ym_ref, g_ref, w_ref, o_ref):
    a = _rms(ys_ref[...], g_ref[:, 0:512]).astype(BF16)
    b = _rms(yw_ref[...], g_ref[:, 512:1024]).astype(BF16)
    c = _rms(ym_ref[...], g_ref[:, 1024:2048]).astype(BF16)
    acc = jnp.dot(a, w_ref[0:512, :], preferred_element_type=F32)
    acc += jnp.dot(b, w_ref[512:1024, :], preferred_element_type=F32)
    acc += jnp.dot(c, w_ref[1024:2048, :], preferred_element_type=F32)
    o_ref[...] = x_ref[...] + acc


def _out_proj(x, ys, yw, ym, gain, w):
    m = x.shape[0]
    tm = min(OUT_TM, m)
    row = lambda i: (i, 0)
    return pl.pallas_call(
        _out_kernel,
        grid=(m // tm,),
        in_specs=[pl.BlockSpec((tm, D_MODEL), row),
                  pl.BlockSpec((tm, SSM_WIDTH), row),
                  pl.BlockSpec((tm, SWA_WIDTH), row),
                  pl.BlockSpec((tm, MLA_WIDTH), row),
                  _const_spec((1, D_MODEL)),
                  _const_spec((D_MODEL, D_MODEL))],
        out_specs=pl.BlockSpec((tm, D_MODEL), row),
        out_shape=jax.ShapeDtypeStruct((m, D_MODEL), F32),
        compiler_params=pltpu.CompilerParams(
            dimension_semantics=("parallel",), vmem_limit_bytes=VMEM_LIMIT),
        name="out_proj",
    )(x, ys, yw, ym, gain, w)


def _final_kernel(x_ref, g_ref, o_ref):
    o_ref[...] = _rms(x_ref[...], g_ref[...])


def _final_norm(x, gain):
    m = x.shape[0]
    tm = min(1024, m)
    return pl.pallas_call(
        _final_kernel,
        grid=(m // tm,),
        in_specs=[pl.BlockSpec((tm, D_MODEL), lambda i: (i, 0)),
                  pl.BlockSpec((1, D_MODEL), lambda i: (0, 0))],
        out_specs=pl.BlockSpec((tm, D_MODEL), lambda i: (i, 0)),
        out_shape=jax.ShapeDtypeStruct((m, D_MODEL), F32),
        name="final_norm",
    )(x, gain)


def _swap_halves(n_heads, head_dim, base):
    half = head_dim // 2
    idx = np.arange(n_heads * head_dim).reshape(n_heads, head_dim)
    idx = np.concatenate([idx[:, half:], idx[:, :half]], axis=1).reshape(-1)
    return base + idx


def _ext_in_weight(w_in):
    o_qs, o_ks, o_vs, o_cq, o_ckv, o_kr = 512, 1024, 1152, 1280, 1792, 2048
    zeros64 = jnp.zeros((w_in.shape[0], 64), w_in.dtype)
    parts = [
        w_in[:, 0:512],
        w_in[:, o_qs:o_qs + 512],
        w_in[:, _swap_halves(SWA_HEADS, 64, o_qs)],
        w_in[:, o_ks:o_ks + 128],
        w_in[:, _swap_halves(2, 64, o_ks)],
        w_in[:, o_vs:o_vs + 128],
        w_in[:, o_cq:o_cq + 512],
        w_in[:, o_ckv:o_ckv + 256],
        w_in[:, o_kr:o_kr + 64], zeros64,
        w_in[:, _swap_halves(1, 64, o_kr)], zeros64,
    ]
    return jnp.concatenate(parts, axis=1).astype(BF16)


def _mla_q_weights(w_uq):
    w = w_uq.reshape(MLA_Q_RANK, MLA_HEADS, MLA_NOPE + MLA_ROPE)
    rope = w[:, :, MLA_NOPE:]
    z = jnp.zeros((MLA_Q_RANK, MLA_HEADS, 64), w.dtype)
    wa = jnp.concatenate([w, z], axis=2).reshape(MLA_Q_RANK, MLA_HEADS * 256)
    swapped = jnp.concatenate([rope[:, :, 32:], rope[:, :, :32]], axis=2)
    wb = jnp.concatenate([swapped, z], axis=2).reshape(MLA_Q_RANK, MLA_HEADS * LANES)
    return wa.astype(BF16), wb.astype(BF16)


def _ssm_params(log_dt, a_re, a_im, b_re, b_im, c_re, c_im, n_sub):
    lr, li = a_re.astype(F32), a_im.astype(F32)
    dt = jnp.exp(log_dt.astype(F32))[:, None]
    mag = jnp.exp(lr * dt)
    abar_r = mag * jnp.cos(li * dt)
    abar_i = mag * jnp.sin(li * dt)
    den = lr * lr + li * li
    nr = abar_r - 1.0
    qr = (nr * lr + abar_i * li) / den
    qi = (abar_i * lr - nr * li) / den
    br, bi = b_re.astype(F32), b_im.astype(F32)
    bbar_r = qr[..., None] * br - qi[..., None] * bi
    bbar_i = qr[..., None] * bi + qi[..., None] * br
    pr, pi = abar_r, abar_i
    for _ in range(int(round(math.log2(n_sub)))):
        pr, pi = pr * pr - pi * pi, 2.0 * pr * pi

    hg = SSM_HALF_GROUPS
    eye = jnp.eye(hg, dtype=F32)

    def lanes(re, im):
        re = re.reshape(2, SSM_HALF_STATES)
        im = im.reshape(2, SSM_HALF_STATES)
        return jnp.concatenate([re, im], axis=1).reshape(1, 4 * SSM_HALF_STATES)

    def b_block(bb):
        bb = bb.reshape(2, hg, SSM_STATE, SSM_GROUP)
        w = jnp.einsum('hgpc,gk->hgckp', bb, eye)
        return w.reshape(2, hg * SSM_GROUP, SSM_HALF_STATES)

    def c_block(cc):
        cc = cc.reshape(2, hg, SSM_GROUP, SSM_STATE)
        w = jnp.einsum('hgcp,gk->hgpkc', cc, eye)
        return w.reshape(2, SSM_HALF_STATES, hg * SSM_GROUP)

    wb = jnp.concatenate([b_block(bbar_r), b_block(bbar_i)], axis=2).astype(BF16)
    wc = jnp.concatenate([c_block(c_re.astype(F32)), -c_block(c_im.astype(F32))], axis=1).astype(BF16)
    return wb, wc, lanes(abar_r, abar_i), lanes(pr, pi)


def _perm_matrices(t):
    n = t // SSM_STREAMS
    r = np.arange(t)
    tok = (r % SSM_STREAMS) * n + r // SSM_STREAMS
    p = np.zeros((t, t), np.float32)
    p[r, tok] = 1.0
    return jnp.asarray(p, BF16), jnp.asarray(p.T, BF16)


def kernel(x, positions, ffn1_norm, ffn1_w_gate, ffn1_w_up, ffn1_w_down, mix_norm, w_in, ssm_log_dt, ssm_a_re, ssm_a_im, ssm_b_re, ssm_b_im, ssm_c_re, ssm_c_im, ssm_d, ssm_w_glu, ssm_b_glu, swa_sinks, mla_q_norm, mla_w_uq, mla_kv_norm, mla_w_ukv, out_norm, w_out, ffn2_norm, ffn2_w_gate, ffn2_w_up, ffn2_w_down, final_norm):
    batch, seq, _ = x.shape
    depth = w_in.shape[0]
    m = batch * seq
    xf = x.reshape(m, D_MODEL)
    cos4, sin4 = _rope_tables(positions)
    t_ssm = min(SSM_T, seq)
    perm, perm_t = _perm_matrices(t_ssm)

    for l in range(depth):
        xf = _ffn(xf, ffn1_norm[l][None, :], ffn1_w_gate[l].astype(BF16), ffn1_w_up[l].astype(BF16),
                  (0.5 * ffn1_w_down[l]).astype(BF16))
        wa, wb_q = _mla_q_weights(mla_w_uq[l])
        u, qs, ks, vs, qc, kc, vm = _mix_in(
            xf, mix_norm[l][None, :], _ext_in_weight(w_in[l]), cos4, sin4,
            mla_q_norm[l][None, :], mla_kv_norm[l][None, :], wa, wb_q, mla_w_ukv[l].astype(BF16),
            batch, seq)
        wb, wc, lam, lamn = _ssm_params(ssm_log_dt[l], ssm_a_re[l], ssm_a_im[l], ssm_b_re[l],
                                        ssm_b_im[l], ssm_c_re[l], ssm_c_im[l], t_ssm // SSM_STREAMS)
        y_ssm = _ssm(u, perm, perm_t, wb, wc, lam, lamn, ssm_d[l][None, :],
                     ssm_w_glu[l].astype(BF16), ssm_b_glu[l][None, :], batch, seq)
        y_swa = _swa(swa_sinks[l], qs, ks, vs, batch, seq)
        y_mla = _mla(qc, kc, vm, batch, seq)
        xf = _out_proj(xf, y_ssm, y_swa, y_mla, out_norm[l][None, :], w_out[l].astype(BF16))
        xf = _ffn(xf, ffn2_norm[l][None, :], ffn2_w_gate[l].astype(BF16), ffn2_w_up[l].astype(BF16),
                  (0.5 * ffn2_w_down[l]).astype(BF16))
    return _final_norm(xf, final_norm[None, :]).reshape(batch, seq, D_MODEL)
```

```python
import functools
import math

import jax
import jax.numpy as jnp
import numpy as np
from jax import lax
from jax.experimental import pallas as pl
from jax.experimental.pallas import tpu as pltpu

F32 = jnp.float32
BF16 = jnp.bfloat16

D_MODEL = 2048
D_FF = 5632
EPS = 1e-6
ROPE_THETA = 10000.0

SSM_GROUP = 16
SSM_WIDTH = 512
SSM_GROUPS = 32
SSM_STATE = 64
SSM_HALF_GROUPS = 16
SSM_HALF_STATES = SSM_HALF_GROUPS * SSM_STATE
SSM_STREAMS = 8

SWA_HEADS = 8
SWA_Q_PER_KV = 4
SWA_HEAD_DIM = 64
SWA_WINDOW = 128
SWA_WIDTH = 512
SWA_KV_WIDTH = 128

MLA_HEADS = 8
MLA_Q_RANK = 512
MLA_KV_RANK = 256
MLA_NOPE = 128
MLA_ROPE = 64
MLA_V = 128
MLA_WIDTH = 1024
MLA_QK_PAD = 256
ROPE_DIM = 64
LANES = 128
LOG2E = 1.4426950408889634

VMEM_LIMIT = 56 * 1024 * 1024

FFN_TM = 512
FFN_TF = 512
MIX_TM = 512
SSM_T = 256
SWA_TQ = 256
MLA_TQ = 512
MLA_HEADS_PER_STEP = 2
OUT_TM = 512
ROPE_TM = 2048

_C_U = 0
_C_QS = 512
_C_QS_ROT = 1024
_C_KS = 1536
_C_KS_ROT = 1664
_C_VS = 1792
_C_CQ = 1920
_C_CKV = 2432
_C_KR = 2688
_C_KR_ROT = 2816
N_EXT = 2944


def _rms(x, gain):
    ms = jnp.mean(x * x, axis=-1, keepdims=True)
    return x * lax.rsqrt(ms + EPS) * gain


def _sigmoid(x):
    return 1.0 / (1.0 + jnp.exp(-x))


def _const_spec(shape):
    nd = len(shape)
    return pl.BlockSpec(shape, lambda *_: (0,) * nd, pipeline_mode=pl.Buffered(1))


def _rope_kernel(pos_ref, freq_ref, sign_ref, cos_ref, sin_ref):
    ang = pos_ref[...].astype(F32) * freq_ref[...]
    cos_ref[...] = jnp.cos(ang)
    sin_ref[...] = jnp.sin(ang) * sign_ref[...]


def _rope_tables(positions):
    m = positions.size
    tm = min(ROPE_TM, m)
    inv_freq = ROPE_THETA ** (-jnp.arange(0, ROPE_DIM, 2, dtype=F32) / ROPE_DIM)
    freq4 = jnp.tile(inv_freq, 4)[None, :]
    sign4 = jnp.tile(jnp.concatenate([-jnp.ones(32, F32), jnp.ones(32, F32)]), 2)[None, :]
    return pl.pallas_call(
        _rope_kernel,
        grid=(m // tm,),
        in_specs=[pl.BlockSpec((tm, 1), lambda i: (i, 0)),
                  pl.BlockSpec((1, LANES), lambda i: (0, 0)),
                  pl.BlockSpec((1, LANES), lambda i: (0, 0))],
        out_specs=[pl.BlockSpec((tm, LANES), lambda i: (i, 0))] * 2,
        out_shape=[jax.ShapeDtypeStruct((m, LANES), F32)] * 2,
        name="rope_tables",
    )(positions.reshape(m, 1), freq4, sign4)


def _ffn_kernel(x_ref, g_ref, wg_ref, wu_ref, wd_ref, o_ref, h_ref):
    f = pl.program_id(1)

    @pl.when(f == 0)
    def _():
        x = x_ref[...]
        h_ref[...] = _rms(x, g_ref[...]).astype(BF16)
        o_ref[...] = x

    h = h_ref[...]
    g = jnp.dot(h, wg_ref[...], preferred_element_type=F32)
    u = jnp.dot(h, wu_ref[...], preferred_element_type=F32)
    a = (g * _sigmoid(g) * u).astype(BF16)
    o_ref[...] += jnp.dot(a, wd_ref[...], preferred_element_type=F32)


def _ffn(x, gain, wg, wu, wd_half):
    m = x.shape[0]
    tm = min(FFN_TM, m)
    return pl.pallas_call(
        _ffn_kernel,
        grid=(m // tm, D_FF // FFN_TF),
        in_specs=[pl.BlockSpec((tm, D_MODEL), lambda i, f: (i, 0)),
                  pl.BlockSpec((1, D_MODEL), lambda i, f: (0, 0)),
                  pl.BlockSpec((D_MODEL, FFN_TF), lambda i, f: (0, f)),
                  pl.BlockSpec((D_MODEL, FFN_TF), lambda i, f: (0, f)),
                  pl.BlockSpec((FFN_TF, D_MODEL), lambda i, f: (f, 0))],
        out_specs=pl.BlockSpec((tm, D_MODEL), lambda i, f: (i, 0)),
        out_shape=jax.ShapeDtypeStruct((m, D_MODEL), F32),
        scratch_shapes=[pltpu.VMEM((tm, D_MODEL), BF16)],
        compiler_params=pltpu.CompilerParams(
            dimension_semantics=("parallel", "arbitrary"), vmem_limit_bytes=VMEM_LIMIT),
        name="ffn",
    )(x, gain, wg, wu, wd_half)


def _mix_in_kernel(x_ref, g_ref, w_ref, cos_ref, sin_ref, qn_ref, kvn_ref, wa_ref, wb_ref, wk_ref, wvt_ref,
                   u_ref, qs_ref, ks_ref, vs_ref, qc_ref, kc_ref, vt_ref):
    h = _rms(x_ref[...], g_ref[...]).astype(BF16)

    def seg(lo, n):
        return jnp.dot(h, w_ref[:, lo:lo + n], preferred_element_type=F32)

    cos4 = cos_ref[...]
    sin4 = sin_ref[...]
    cos512 = jnp.concatenate([cos4] * 4, axis=1)
    sin512 = jnp.concatenate([sin4] * 4, axis=1)

    u_ref[...] = seg(_C_U, SSM_WIDTH)
    qs = seg(_C_QS, SWA_WIDTH) * cos512 + seg(_C_QS_ROT, SWA_WIDTH) * sin512
    qs_ref[...] = (qs * (SWA_HEAD_DIM ** -0.5)).astype(BF16)
    ks_ref[...] = (seg(_C_KS, SWA_KV_WIDTH) * cos4 + seg(_C_KS_ROT, SWA_KV_WIDTH) * sin4).astype(BF16)
    vs_ref[...] = seg(_C_VS, SWA_KV_WIDTH).astype(BF16)

    cqn = _rms(seg(_C_CQ, MLA_Q_RANK), qn_ref[...]).astype(BF16)
    ckvn = _rms(seg(_C_CKV, MLA_KV_RANK), kvn_ref[...]).astype(BF16)
    kr = (seg(_C_KR, LANES) * cos4 + seg(_C_KR_ROT, LANES) * sin4).astype(BF16)

    scale = (MLA_NOPE + MLA_ROPE) ** -0.5 * LOG2E
    for hd in range(MLA_HEADS):
        qa = jnp.dot(cqn, wa_ref[:, hd * 256:(hd + 1) * 256], preferred_element_type=F32)
        qb = jnp.dot(cqn, wb_ref[:, hd * LANES:(hd + 1) * LANES], preferred_element_type=F32)
        qc_ref[hd, :, 0:LANES] = (qa[:, :LANES] * scale).astype(BF16)
        qc_ref[hd, :, LANES:2 * LANES] = ((qa[:, LANES:] * cos4 + qb * sin4) * scale).astype(BF16)
        kn = jnp.dot(ckvn, wk_ref[:, hd * LANES:(hd + 1) * LANES], preferred_element_type=F32)
        kc_ref[hd, :, 0:LANES] = kn.astype(BF16)
        kc_ref[hd, :, LANES:2 * LANES] = kr
        vt = lax.dot_general(wvt_ref[hd * MLA_V:(hd + 1) * MLA_V, :], ckvn, (((1,), (1,)), ((), ())),
                             preferred_element_type=F32)
        vt_ref[hd] = vt.astype(BF16)


def _mix_in(x, gain, w_ext, cos4, sin4, qn, kvn, wa, wb, wk, wvt, batch, seq):
    m = x.shape[0]
    tm = min(MIX_TM, seq)
    nt = seq // tm
    row = lambda b, i: (b * nt + i, 0)
    head_spec = lambda w: pl.BlockSpec((None, MLA_HEADS, tm, w), lambda b, i: (b, 0, i, 0))
    return pl.pallas_call(
        _mix_in_kernel,
        grid=(batch, nt),
        in_specs=[pl.BlockSpec((tm, D_MODEL), row),
                  _const_spec((1, D_MODEL)),
                  _const_spec((D_MODEL, N_EXT)),
                  pl.BlockSpec((tm, LANES), row),
                  pl.BlockSpec((tm, LANES), row),
                  _const_spec((1, MLA_Q_RANK)),
                  _const_spec((1, MLA_KV_RANK)),
                  _const_spec((MLA_Q_RANK, MLA_HEADS * 256)),
                  _const_spec((MLA_Q_RANK, MLA_HEADS * LANES)),
                  _const_spec((MLA_KV_RANK, MLA_HEADS * MLA_NOPE)),
                  _const_spec((MLA_HEADS * MLA_V, MLA_KV_RANK))],
        out_specs=[pl.BlockSpec((tm, SSM_WIDTH), row),
                   pl.BlockSpec((tm, SWA_WIDTH), row),
                   pl.BlockSpec((tm, SWA_KV_WIDTH), row),
                   pl.BlockSpec((tm, SWA_KV_WIDTH), row),
                   head_spec(MLA_QK_PAD), head_spec(MLA_QK_PAD),
                   pl.BlockSpec((None, MLA_HEADS, MLA_V, tm), lambda b, i: (b, 0, 0, i))],
        out_shape=[jax.ShapeDtypeStruct((m, SSM_WIDTH), F32),
                   jax.ShapeDtypeStruct((m, SWA_WIDTH), BF16),
                   jax.ShapeDtypeStruct((m, SWA_KV_WIDTH), BF16),
                   jax.ShapeDtypeStruct((m, SWA_KV_WIDTH), BF16),
                   jax.ShapeDtypeStruct((batch, MLA_HEADS, seq, MLA_QK_PAD), BF16),
                   jax.ShapeDtypeStruct((batch, MLA_HEADS, seq, MLA_QK_PAD), BF16),
                   jax.ShapeDtypeStruct((batch, MLA_HEADS, MLA_V, seq), BF16)],
        compiler_params=pltpu.CompilerParams(
            dimension_semantics=("parallel", "parallel"), vmem_limit_bytes=VMEM_LIMIT),
        name="mix_in",
    )(x, gain, w_ext, cos4, sin4, qn, kvn, wa, wb, wk, wvt)


def _ssm_kernel(u_ref, p_ref, pt_ref, wb_ref, wc_ref, lam_ref, lamn_ref, d_ref, wglu_ref, bglu_ref,
                y_ref, s_ref, carry_ref, f_ref, *, t, lane_blk):
    n = t // SSM_STREAMS
    hs = SSM_HALF_STATES

    @pl.when(pl.program_id(1) == 0)
    def _():
        carry_ref[...] = jnp.zeros_like(carry_ref)

    u = u_ref[...]
    up = jnp.dot(p_ref[...], u.astype(BF16), preferred_element_type=F32).astype(BF16)
    for hf in range(2):
        s_ref[:, hf * 2 * hs:(hf + 1) * 2 * hs] = jnp.dot(
            up[:, hf * 256:(hf + 1) * 256], wb_ref[hf], preferred_element_type=F32)

    row_id = lax.broadcasted_iota(jnp.int32, (SSM_STREAMS, lane_blk), 0)
    for hf in range(2):
        for lb in range(hs // lane_blk):
            re0 = hf * 2 * hs + lb * lane_blk
            im0 = re0 + hs
            re_sl = slice(re0, re0 + lane_blk)
            im_sl = slice(im0, im0 + lane_blk)
            ar = jnp.broadcast_to(lam_ref[:, re_sl], (SSM_STREAMS, lane_blk))
            ai = jnp.broadcast_to(lam_ref[:, im_sl], (SSM_STREAMS, lane_blk))

            def scan(x0r, x0i, store):
                def body(i, c):
                    xr, xi = c
                    r0 = pl.multiple_of(i * SSM_STREAMS, SSM_STREAMS)
                    br = s_ref[pl.ds(r0, SSM_STREAMS), re_sl]
                    bi = s_ref[pl.ds(r0, SSM_STREAMS), im_sl]
                    nr = ar * xr - ai * xi + br
                    ni = ar * xi + ai * xr + bi
                    if store:
                        s_ref[pl.ds(r0, SSM_STREAMS), re_sl] = nr
                        s_ref[pl.ds(r0, SSM_STREAMS), im_sl] = ni
                    return nr, ni
                return lax.fori_loop(0, n, body, (x0r, x0i))

            zero = jnp.zeros((SSM_STREAMS, lane_blk), F32)
            fr, fi = scan(zero, zero, False)
            f_ref[:, 0:lane_blk] = fr
            f_ref[:, lane_blk:2 * lane_blk] = fi
            pr = lamn_ref[:, re_sl]
            pi = lamn_ref[:, im_sl]
            sr = carry_ref[:, re_sl]
            si = carry_ref[:, im_sl]
            s0r = jnp.zeros((SSM_STREAMS, lane_blk), F32)
            s0i = jnp.zeros((SSM_STREAMS, lane_blk), F32)
            for k in range(SSM_STREAMS):
                s0r = jnp.where(row_id == k, jnp.broadcast_to(sr, s0r.shape), s0r)
                s0i = jnp.where(row_id == k, jnp.broadcast_to(si, s0i.shape), s0i)
                er = f_ref[k:k + 1, 0:lane_blk]
                ei = f_ref[k:k + 1, lane_blk:2 * lane_blk]
                sr, si = pr * sr - pi * si + er, pr * si + pi * sr + ei
            carry_ref[:, re_sl] = sr
            carry_ref[:, im_sl] = si
            scan(s0r, s0i, True)

    yp = []
    for hf in range(2):
        xb = s_ref[:, hf * 2 * hs:(hf + 1) * 2 * hs].astype(BF16)
        yp.append(jnp.dot(xb, wc_ref[hf], preferred_element_type=F32))
    yp = jnp.concatenate(yp, axis=1)
    y_hi = yp.astype(BF16)
    y_lo = (yp - y_hi.astype(F32)).astype(BF16)
    pt = pt_ref[...]
    y = (jnp.dot(pt, y_hi, preferred_element_type=F32)
         + jnp.dot(pt, y_lo, preferred_element_type=F32))
    y = y + d_ref[...] * u
    c0 = math.sqrt(2.0 / math.pi)
    y = 0.5 * y * (1.0 + jnp.tanh(c0 * (y + 0.044715 * (y * y * y))))
    z = jnp.dot(y.astype(BF16), wglu_ref[...], preferred_element_type=F32) + bglu_ref[...]
    y_ref[...] = y * _sigmoid(z)


def _ssm(u, perm, perm_t, wb, wc, lam, lamn, d_skip, wglu, bglu, batch, seq):
    m = u.shape[0]
    t = min(SSM_T, seq)
    nt = seq // t
    row = lambda b, i: (b * nt + i, 0)
    kern = functools.partial(_ssm_kernel, t=t, lane_blk=512)
    return pl.pallas_call(
        kern,
        grid=(batch, nt),
        in_specs=[pl.BlockSpec((t, SSM_WIDTH), row),
                  _const_spec((t, t)), _const_spec((t, t)),
                  _const_spec((2, 256, 2 * SSM_HALF_STATES)),
                  _const_spec((2, 2 * SSM_HALF_STATES, 256)),
                  _const_spec((1, 4 * SSM_HALF_STATES)),
                  _const_spec((1, 4 * SSM_HALF_STATES)),
                  _const_spec((1, SSM_WIDTH)),
                  _const_spec((SSM_WIDTH, SSM_WIDTH)),
                  _const_spec((1, SSM_WIDTH))],
        out_specs=pl.BlockSpec((t, SSM_WIDTH), row),
        out_shape=jax.ShapeDtypeStruct((m, SSM_WIDTH), F32),
        scratch_shapes=[pltpu.VMEM((t, 4 * SSM_HALF_STATES), F32),
                        pltpu.VMEM((1, 4 * SSM_HALF_STATES), F32),
                        pltpu.VMEM((SSM_STREAMS, 1024), F32)],
        compiler_params=pltpu.CompilerParams(
            dimension_semantics=("parallel", "arbitrary"), vmem_limit_bytes=VMEM_LIMIT),
        name="ssm",
    )(u, perm, perm_t, wb, wc, lam, lamn, d_skip, wglu, bglu)


def _swa_kernel(sink_ref, q_ref, k_ref, v_ref, o_ref, *, tq):
    i = pl.program_id(1)
    tk = tq + SWA_WINDOW
    start = pl.multiple_of(jnp.maximum(i * tq - SWA_WINDOW, 0), SWA_WINDOW)
    kk = k_ref[pl.ds(start, tk), :]
    vv = v_ref[pl.ds(start, tk), :]
    qpos = i * tq + lax.broadcasted_iota(jnp.int32, (tq, tk), 0)
    kpos = start + lax.broadcasted_iota(jnp.int32, (tq, tk), 1)
    dist = qpos - kpos
    mask = (dist >= 0) & (dist < SWA_WINDOW)
    outs = []
    for hd in range(SWA_HEADS):
        g = hd // SWA_Q_PER_KV
        qh = q_ref[:, hd * SWA_HEAD_DIM:(hd + 1) * SWA_HEAD_DIM]
        kh = kk[:, g * SWA_HEAD_DIM:(g + 1) * SWA_HEAD_DIM]
        vh = vv[:, g * SWA_HEAD_DIM:(g + 1) * SWA_HEAD_DIM]
        s = lax.dot_general(qh, kh, (((1,), (1,)), ((), ())), preferred_element_type=F32)
        s = jnp.where(mask, s, -jnp.inf)
        sink = sink_ref[hd]
        mx = jnp.maximum(jnp.max(s, axis=-1, keepdims=True), sink)
        e = jnp.exp(s - mx)
        den = jnp.sum(e, axis=-1, keepdims=True) + jnp.exp(sink - mx)
        p = (e / den).astype(BF16)
        outs.append(jnp.dot(p, vh, preferred_element_type=F32))
    o_ref[...] = jnp.concatenate(outs, axis=1)


def _swa(sinks, q, k, v, batch, seq):
    m = q.shape[0]
    tq = min(SWA_TQ, seq)
    nt = seq // tq
    kern = functools.partial(_swa_kernel, tq=tq)
    k3 = k.reshape(batch, seq, SWA_KV_WIDTH)
    v3 = v.reshape(batch, seq, SWA_KV_WIDTH)
    return pl.pallas_call(
        kern,
        grid=(batch, nt),
        in_specs=[pl.BlockSpec(memory_space=pltpu.SMEM),
                  pl.BlockSpec((tq, SWA_WIDTH), lambda b, i: (b * nt + i, 0)),
                  pl.BlockSpec((None, seq, SWA_KV_WIDTH), lambda b, i: (b, 0, 0)),
                  pl.BlockSpec((None, seq, SWA_KV_WIDTH), lambda b, i: (b, 0, 0))],
        out_specs=pl.BlockSpec((tq, SWA_WIDTH), lambda b, i: (b * nt + i, 0)),
        out_shape=jax.ShapeDtypeStruct((m, SWA_WIDTH), F32),
        compiler_params=pltpu.CompilerParams(
            dimension_semantics=("parallel", "parallel"), vmem_limit_bytes=VMEM_LIMIT),
        name="swa",
    )(sinks, q, k3, v3)


def _mla_kernel(q_ref, k_ref, vt_ref, o_ref, m_ref, l_ref, acc_ref, *, tq, hp):
    qi = pl.program_id(2)
    m_ref[...] = jnp.full_like(m_ref, -jnp.inf)
    l_ref[...] = jnp.zeros_like(l_ref)
    acc_ref[...] = jnp.zeros_like(acc_ref)

    def block(j, masked):
        r0 = pl.multiple_of(j * tq, tq)
        for hh in range(hp):
            k = k_ref[hh, pl.ds(r0, tq), :]
            vt = vt_ref[hh, :, pl.ds(r0, tq)]
            st = lax.dot_general(k, q_ref[hh], (((1,), (1,)), ((), ())),
                                 preferred_element_type=F32)
            if masked:
                key = lax.broadcasted_iota(jnp.int32, (tq, tq), 0)
                qry = lax.broadcasted_iota(jnp.int32, (tq, tq), 1)
                st = jnp.where(key <= qry, st, -jnp.inf)
            m_old = m_ref[hh]
            m_new = jnp.maximum(m_old, jnp.max(st, axis=0, keepdims=True))
            alpha = jnp.exp2(m_old - m_new)
            p = jnp.exp2(st - m_new)
            l_ref[hh] = alpha * l_ref[hh] + jnp.sum(p, axis=0, keepdims=True)
            acc_ref[hh] = alpha * acc_ref[hh] + jnp.dot(vt, p.astype(BF16), preferred_element_type=F32)
            m_ref[hh] = m_new

    def body(j, c):
        block(j, False)
        return c

    lax.fori_loop(0, qi, body, 0)
    block(qi, True)
    for hh in range(hp):
        o_ref[:, hh * MLA_V:(hh + 1) * MLA_V] = (acc_ref[hh] / l_ref[hh]).T


def _mla(qc, kc, vt, batch, seq):
    tq = min(MLA_TQ, seq)
    nt = seq // tq
    hp = MLA_HEADS_PER_STEP
    kern = functools.partial(_mla_kernel, tq=tq, hp=hp)
    return pl.pallas_call(
        kern,
        grid=(batch, MLA_HEADS // hp, nt),
        in_specs=[pl.BlockSpec((None, hp, tq, MLA_QK_PAD), lambda b, h, i: (b, h, i, 0)),
                  pl.BlockSpec((None, hp, seq, MLA_QK_PAD), lambda b, h, i: (b, h, 0, 0)),
                  pl.BlockSpec((None, hp, MLA_V, seq), lambda b, h, i: (b, h, 0, 0))],
        out_specs=pl.BlockSpec((tq, hp * MLA_V), lambda b, h, i: (b * nt + i, h)),
        out_shape=jax.ShapeDtypeStruct((batch * seq, MLA_WIDTH), F32),
        scratch_shapes=[pltpu.VMEM((hp, 1, tq), F32), pltpu.VMEM((hp, 1, tq), F32),
                        pltpu.VMEM((hp, MLA_V, tq), F32)],
        compiler_params=pltpu.CompilerParams(
            dimension_semantics=("parallel", "parallel", "parallel"), vmem_limit_bytes=VMEM_LIMIT),
        name="mla",
    )(qc, kc, vt)


def _out_kernel(x_ref, ys_ref, yw_ref, ym_ref, g_ref, w_ref, o_ref):
    a = _rms(ys_ref[...], g_ref[:, 0:512]).astype(BF16)
    b = _rms(yw_ref[...], g_ref[:, 512:1024]).astype(BF16)
    c = _rms(ym_ref[...], g_ref[:, 1024:2048]).astype(BF16)
    acc = jnp.dot(a, w_ref[0:512, :], preferred_element_type=F32)
    acc += jnp.dot(b, w_ref[512:1024, :], preferred_element_type=F32)
    acc += jnp.dot(c, w_ref[1024:2048, :], preferred_element_type=F32)
    o_ref[...] = x_ref[...] + acc


def _out_proj(x, ys, yw, ym, gain, w):
    m = x.shape[0]
    tm = min(OUT_TM, m)
    row = lambda i: (i, 0)
    return pl.pallas_call(
        _out_kernel,
        grid=(m // tm,),
        in_specs=[pl.BlockSpec((tm, D_MODEL), row),
                  pl.BlockSpec((tm, SSM_WIDTH), row),
                  pl.BlockSpec((tm, SWA_WIDTH), row),
                  pl.BlockSpec((tm, MLA_WIDTH), row),
                  _const_spec((1, D_MODEL)),
                  _const_spec((D_MODEL, D_MODEL))],
        out_specs=pl.BlockSpec((tm, D_MODEL), row),
        out_shape=jax.ShapeDtypeStruct((m, D_MODEL), F32),
        compiler_params=pltpu.CompilerParams(
            dimension_semantics=("parallel",), vmem_limit_bytes=VMEM_LIMIT),
        name="out_proj",
    )(x, ys, yw, ym, gain, w)


def _final_kernel(x_ref, g_ref, o_ref):
    o_ref[...] = _rms(x_ref[...], g_ref[...])


def _final_norm(x, gain):
    m = x.shape[0]
    tm = min(1024, m)
    return pl.pallas_call(
        _final_kernel,
        grid=(m // tm,),
        in_specs=[pl.BlockSpec((tm, D_MODEL), lambda i: (i, 0)),
                  pl.BlockSpec((1, D_MODEL), lambda i: (0, 0))],
        out_specs=pl.BlockSpec((tm, D_MODEL), lambda i: (i, 0)),
        out_shape=jax.ShapeDtypeStruct((m, D_MODEL), F32),
        name="final_norm",
    )(x, gain)


def _swap_halves(n_heads, head_dim, base):
    half = head_dim // 2
    idx = np.arange(n_heads * head_dim).reshape(n_heads, head_dim)
    idx = np.concatenate([idx[:, half:], idx[:, :half]], axis=1).reshape(-1)
    return base + idx


def _ext_in_weight(w_in):
    o_qs, o_ks, o_vs, o_cq, o_ckv, o_kr = 512, 1024, 1152, 1280, 1792, 2048
    zeros64 = jnp.zeros((w_in.shape[0], 64), w_in.dtype)
    parts = [
        w_in[:, 0:512],
        w_in[:, o_qs:o_qs + 512],
        w_in[:, _swap_halves(SWA_HEADS, 64, o_qs)],
        w_in[:, o_ks:o_ks + 128],
        w_in[:, _swap_halves(2, 64, o_ks)],
        w_in[:, o_vs:o_vs + 128],
        w_in[:, o_cq:o_cq + 512],
        w_in[:, o_ckv:o_ckv + 256],
        w_in[:, o_kr:o_kr + 64], zeros64,
        w_in[:, _swap_halves(1, 64, o_kr)], zeros64,
    ]
    return jnp.concatenate(parts, axis=1).astype(BF16)


def _mla_q_weights(w_uq):
    w = w_uq.reshape(MLA_Q_RANK, MLA_HEADS, MLA_NOPE + MLA_ROPE)
    rope = w[:, :, MLA_NOPE:]
    z = jnp.zeros((MLA_Q_RANK, MLA_HEADS, 64), w.dtype)
    wa = jnp.concatenate([w, z], axis=2).reshape(MLA_Q_RANK, MLA_HEADS * 256)
    swapped = jnp.concatenate([rope[:, :, 32:], rope[:, :, :32]], axis=2)
    wb = jnp.concatenate([swapped, z], axis=2).reshape(MLA_Q_RANK, MLA_HEADS * LANES)
    return wa.astype(BF16), wb.astype(BF16)


def _mla_kv_weights(w_ukv):
    w = w_ukv.reshape(MLA_KV_RANK, MLA_HEADS, MLA_NOPE + MLA_V)
    wk = w[:, :, :MLA_NOPE].reshape(MLA_KV_RANK, MLA_HEADS * MLA_NOPE)
    wvt = w[:, :, MLA_NOPE:].reshape(MLA_KV_RANK, MLA_HEADS * MLA_V).T
    return wk.astype(BF16), wvt.astype(BF16)


def _ssm_params(log_dt, a_re, a_im, b_re, b_im, c_re, c_im, n_sub):
    lr, li = a_re.astype(F32), a_im.astype(F32)
    dt = jnp.exp(log_dt.astype(F32))[:, None]
    mag = jnp.exp(lr * dt)
    abar_r = mag * jnp.cos(li * dt)
    abar_i = mag * jnp.sin(li * dt)
    den = lr * lr + li * li
    nr = abar_r - 1.0
    qr = (nr * lr + abar_i * li) / den
    qi = (abar_i * lr - nr * li) / den
    br, bi = b_re.astype(F32), b_im.astype(F32)
    bbar_r = qr[..., None] * br - qi[..., None] * bi
    bbar_i = qr[..., None] * bi + qi[..., None] * br
    pr, pi = abar_r, abar_i
    for _ in range(int(round(math.log2(n_sub)))):
        pr, pi = pr * pr - pi * pi, 2.0 * pr * pi

    hg = SSM_HALF_GROUPS
    eye = jnp.eye(hg, dtype=F32)

    def lanes(re, im):
        re = re.reshape(2, SSM_HALF_STATES)
        im = im.reshape(2, SSM_HALF_STATES)
        return jnp.concatenate([re, im], axis=1).reshape(1, 4 * SSM_HALF_STATES)

    def b_block(bb):
        bb = bb.reshape(2, hg, SSM_STATE, SSM_GROUP)
        w = jnp.einsum('hgpc,gk->hgckp', bb, eye)
        return w.reshape(2, hg * SSM_GROUP, SSM_HALF_STATES)

    def c_block(cc):
        cc = cc.reshape(2, hg, SSM_GROUP, SSM_STATE)
        w = jnp.einsum('hgcp,gk->hgpkc', cc, eye)
        return w.reshape(2, SSM_HALF_STATES, hg * SSM_GROUP)

    wb = jnp.concatenate([b_block(bbar_r), b_block(bbar_i)], axis=2).astype(BF16)
    wc = jnp.concatenate([c_block(c_re.astype(F32)), -c_block(c_im.astype(F32))], axis=1).astype(BF16)
    return wb, wc, lanes(abar_r, abar_i), lanes(pr, pi)


def _perm_matrices(t):
    n = t // SSM_STREAMS
    r = np.arange(t)
    tok = (r % SSM_STREAMS) * n + r // SSM_STREAMS
    p = np.zeros((t, t), np.float32)
    p[r, tok] = 1.0
    return jnp.asarray(p, BF16), jnp.asarray(p.T, BF16)


def kernel(x, positions, ffn1_norm, ffn1_w_gate, ffn1_w_up, ffn1_w_down, mix_norm, w_in, ssm_log_dt, ssm_a_re, ssm_a_im, ssm_b_re, ssm_b_im, ssm_c_re, ssm_c_im, ssm_d, ssm_w_glu, ssm_b_glu, swa_sinks, mla_q_norm, mla_w_uq, mla_kv_norm, mla_w_ukv, out_norm, w_out, ffn2_norm, ffn2_w_gate, ffn2_w_up, ffn2_w_down, final_norm):
    batch, seq, _ = x.shape
    depth = w_in.shape[0]
    m = batch * seq
    xf = x.reshape(m, D_MODEL)
    cos4, sin4 = _rope_tables(positions)
    t_ssm = min(SSM_T, seq)
    perm, perm_t = _perm_matrices(t_ssm)

    for l in range(depth):
        xf = _ffn(xf, ffn1_norm[l][None, :], ffn1_w_gate[l].astype(BF16), ffn1_w_up[l].astype(BF16),
                  (0.5 * ffn1_w_down[l]).astype(BF16))
        wa, wb_q = _mla_q_weights(mla_w_uq[l])
        wk, wvt = _mla_kv_weights(mla_w_ukv[l])
        u, qs, ks, vs, qc, kc, vm = _mix_in(
            xf, mix_norm[l][None, :], _ext_in_weight(w_in[l]), cos4, sin4,
            mla_q_norm[l][None, :], mla_kv_norm[l][None, :], wa, wb_q, wk, wvt, batch, seq)
        wb, wc, lam, lamn = _ssm_params(ssm_log_dt[l], ssm_a_re[l], ssm_a_im[l], ssm_b_re[l],
                                        ssm_b_im[l], ssm_c_re[l], ssm_c_im[l], t_ssm // SSM_STREAMS)
        y_ssm = _ssm(u, perm, perm_t, wb, wc, lam, lamn, ssm_d[l][None, :],
                     ssm_w_glu[l].astype(BF16), ssm_b_glu[l][None, :], batch, seq)
        y_swa = _swa(swa_sinks[l], qs, ks, vs, batch, seq)
        y_mla = _mla(qc, kc, vm, batch, seq)
        xf = _out_proj(xf, y_ssm, y_swa, y_mla, out_norm[l][None, :], w_out[l].astype(BF16))
        xf = _ffn(xf, ffn2_norm[l][None, :], ffn2_w_gate[l].astype(BF16), ffn2_w_up[l].astype(BF16),
                  (0.5 * ffn2_w_down[l]).astype(BF16))
    return _final_norm(xf, final_norm[None, :]).reshape(batch, seq, D_MODEL)
```

```python
import functools
import math

import jax
import jax.numpy as jnp
import numpy as np
from jax import lax
from jax.experimental import pallas as pl
from jax.experimental.pallas import tpu as pltpu

F32 = jnp.float32
BF16 = jnp.bfloat16

D_MODEL = 2048
D_FF = 5632
EPS = 1e-6
ROPE_THETA = 10000.0

SSM_GROUP = 16
SSM_WIDTH = 512
SSM_GROUPS = 32
SSM_STATE = 64
SSM_HALF_GROUPS = 16
SSM_HALF_STATES = SSM_HALF_GROUPS * SSM_STATE
SSM_STREAMS = 8

SWA_HEADS = 8
SWA_Q_PER_KV = 4
SWA_HEAD_DIM = 64
SWA_WINDOW = 128
SWA_WIDTH = 512
SWA_KV_WIDTH = 128

MLA_HEADS = 8
MLA_Q_RANK = 512
MLA_KV_RANK = 256
MLA_NOPE = 128
MLA_ROPE = 64
MLA_V = 128
MLA_WIDTH = 1024
MLA_QK_PAD = 256
ROPE_DIM = 64
LANES = 128
LOG2E = 1.4426950408889634

VMEM_LIMIT = 56 * 1024 * 1024

FFN_TM = 1024
FFN_TF = 256
MIX_TM = 512
SSM_T = 256
SWA_TQ = 256
MLA_TQ = 512
MLA_HEADS_PER_STEP = 2
OUT_TM = 512
ROPE_TM = 2048

_C_U = 0
_C_QS = 512
_C_QS_ROT = 1024
_C_KS = 1536
_C_KS_ROT = 1664
_C_VS = 1792
_C_CQ = 1920
_C_CKV = 2432
_C_KR = 2688
_C_KR_ROT = 2816
N_EXT = 2944


def _rms(x, gain):
    ms = jnp.mean(x * x, axis=-1, keepdims=True)
    return x * lax.rsqrt(ms + EPS) * gain


def _sigmoid(x):
    return 1.0 / (1.0 + jnp.exp(-x))


def _const_spec(shape):
    nd = len(shape)
    return pl.BlockSpec(shape, lambda *_: (0,) * nd, pipeline_mode=pl.Buffered(1))


def _rope_kernel(pos_ref, freq_ref, sign_ref, cos_ref, sin_ref):
    ang = pos_ref[...].astype(F32) * freq_ref[...]
    cos_ref[...] = jnp.cos(ang)
    sin_ref[...] = jnp.sin(ang) * sign_ref[...]


def _rope_tables(positions):
    m = positions.size
    tm = min(ROPE_TM, m)
    inv_freq = ROPE_THETA ** (-jnp.arange(0, ROPE_DIM, 2, dtype=F32) / ROPE_DIM)
    freq4 = jnp.tile(inv_freq, 4)[None, :]
    sign4 = jnp.tile(jnp.concatenate([-jnp.ones(32, F32), jnp.ones(32, F32)]), 2)[None, :]
    return pl.pallas_call(
        _rope_kernel,
        grid=(m // tm,),
        in_specs=[pl.BlockSpec((tm, 1), lambda i: (i, 0)),
                  pl.BlockSpec((1, LANES), lambda i: (0, 0)),
                  pl.BlockSpec((1, LANES), lambda i: (0, 0))],
        out_specs=[pl.BlockSpec((tm, LANES), lambda i: (i, 0))] * 2,
        out_shape=[jax.ShapeDtypeStruct((m, LANES), F32)] * 2,
        name="rope_tables",
    )(positions.reshape(m, 1), freq4, sign4)


def _ffn_kernel(x_ref, g_ref, wg_ref, wu_ref, wd_ref, o_ref, h_ref):
    f = pl.program_id(1)

    @pl.when(f == 0)
    def _():
        x = x_ref[...]
        h_ref[...] = _rms(x, g_ref[...]).astype(BF16)
        o_ref[...] = x

    h = h_ref[...]
    g = jnp.dot(h, wg_ref[...].astype(BF16), preferred_element_type=F32)
    u = jnp.dot(h, wu_ref[...].astype(BF16), preferred_element_type=F32)
    a = (g * _sigmoid(g) * (0.5 * u)).astype(BF16)
    o_ref[...] += jnp.dot(a, wd_ref[...].astype(BF16), preferred_element_type=F32)


def _ffn(x, gain, wg, wu, wd, layer):
    m = x.shape[0]
    tm = min(FFN_TM, m)
    return pl.pallas_call(
        _ffn_kernel,
        grid=(m // tm, D_FF // FFN_TF),
        in_specs=[pl.BlockSpec((tm, D_MODEL), lambda i, f: (i, 0)),
                  pl.BlockSpec((None, 1, D_MODEL), lambda i, f: (layer, 0, 0)),
                  pl.BlockSpec((None, D_MODEL, FFN_TF), lambda i, f: (layer, 0, f)),
                  pl.BlockSpec((None, D_MODEL, FFN_TF), lambda i, f: (layer, 0, f)),
                  pl.BlockSpec((None, FFN_TF, D_MODEL), lambda i, f: (layer, f, 0))],
        out_specs=pl.BlockSpec((tm, D_MODEL), lambda i, f: (i, 0)),
        out_shape=jax.ShapeDtypeStruct((m, D_MODEL), F32),
        scratch_shapes=[pltpu.VMEM((tm, D_MODEL), BF16)],
        compiler_params=pltpu.CompilerParams(
            dimension_semantics=("parallel", "arbitrary"), vmem_limit_bytes=VMEM_LIMIT),
        name="ffn",
    )(x, gain, wg, wu, wd)


def _mix_in_kernel(x_ref, g_ref, w_ref, cos_ref, sin_ref, qn_ref, kvn_ref, wa_ref, wb_ref, wk_ref, wvt_ref,
                   u_ref, qs_ref, ks_ref, vs_ref, qc_ref, kc_ref, vt_ref):
    h = _rms(x_ref[...], g_ref[...]).astype(BF16)

    def seg(lo, n):
        return jnp.dot(h, w_ref[:, lo:lo + n], preferred_element_type=F32)

    cos4 = cos_ref[...]
    sin4 = sin_ref[...]
    cos512 = jnp.concatenate([cos4] * 4, axis=1)
    sin512 = jnp.concatenate([sin4] * 4, axis=1)

    u_ref[...] = seg(_C_U, SSM_WIDTH)
    qs = seg(_C_QS, SWA_WIDTH) * cos512 + seg(_C_QS_ROT, SWA_WIDTH) * sin512
    qs_ref[...] = (qs * (SWA_HEAD_DIM ** -0.5)).astype(BF16)
    ks_ref[...] = (seg(_C_KS, SWA_KV_WIDTH) * cos4 + seg(_C_KS_ROT, SWA_KV_WIDTH) * sin4).astype(BF16)
    vs_ref[...] = seg(_C_VS, SWA_KV_WIDTH).astype(BF16)

    cqn = _rms(seg(_C_CQ, MLA_Q_RANK), qn_ref[...]).astype(BF16)
    ckvn = _rms(seg(_C_CKV, MLA_KV_RANK), kvn_ref[...]).astype(BF16)
    kr = (seg(_C_KR, LANES) * cos4 + seg(_C_KR_ROT, LANES) * sin4).astype(BF16)

    scale = (MLA_NOPE + MLA_ROPE) ** -0.5 * LOG2E
    for hd in range(MLA_HEADS):
        qa = jnp.dot(cqn, wa_ref[:, hd * 256:(hd + 1) * 256], preferred_element_type=F32)
        qb = jnp.dot(cqn, wb_ref[:, hd * LANES:(hd + 1) * LANES], preferred_element_type=F32)
        qc_ref[hd, :, 0:LANES] = (qa[:, :LANES] * scale).astype(BF16)
        qc_ref[hd, :, LANES:2 * LANES] = ((qa[:, LANES:] * cos4 + qb * sin4) * scale).astype(BF16)
        kn = jnp.dot(ckvn, wk_ref[:, hd * LANES:(hd + 1) * LANES], preferred_element_type=F32)
        kc_ref[hd, :, 0:LANES] = kn.astype(BF16)
        kc_ref[hd, :, LANES:2 * LANES] = kr
        vt = lax.dot_general(wvt_ref[hd * MLA_V:(hd + 1) * MLA_V, :], ckvn, (((1,), (1,)), ((), ())),
                             preferred_element_type=F32)
        vt_ref[hd] = vt.astype(BF16)


def _mix_in(x, gain, w_ext, cos4, sin4, qn, kvn, wa, wb, wk, wvt, batch, seq):
    m = x.shape[0]
    tm = min(MIX_TM, seq)
    nt = seq // tm
    row = lambda b, i: (b * nt + i, 0)
    head_spec = lambda w: pl.BlockSpec((None, MLA_HEADS, tm, w), lambda b, i: (b, 0, i, 0))
    return pl.pallas_call(
        _mix_in_kernel,
        grid=(batch, nt),
        in_specs=[pl.BlockSpec((tm, D_MODEL), row),
                  _const_spec((1, D_MODEL)),
                  _const_spec((D_MODEL, N_EXT)),
                  pl.BlockSpec((tm, LANES), row),
                  pl.BlockSpec((tm, LANES), row),
                  _const_spec((1, MLA_Q_RANK)),
                  _const_spec((1, MLA_KV_RANK)),
                  _const_spec((MLA_Q_RANK, MLA_HEADS * 256)),
                  _const_spec((MLA_Q_RANK, MLA_HEADS * LANES)),
                  _const_spec((MLA_KV_RANK, MLA_HEADS * MLA_NOPE)),
                  _const_spec((MLA_HEADS * MLA_V, MLA_KV_RANK))],
        out_specs=[pl.BlockSpec((tm, SSM_WIDTH), row),
                   pl.BlockSpec((tm, SWA_WIDTH), row),
                   pl.BlockSpec((tm, SWA_KV_WIDTH), row),
                   pl.BlockSpec((tm, SWA_KV_WIDTH), row),
                   head_spec(MLA_QK_PAD), head_spec(MLA_QK_PAD),
                   pl.BlockSpec((None, MLA_HEADS, MLA_V, tm), lambda b, i: (b, 0, 0, i))],
        out_shape=[jax.ShapeDtypeStruct((m, SSM_WIDTH), F32),
                   jax.ShapeDtypeStruct((m, SWA_WIDTH), BF16),
                   jax.ShapeDtypeStruct((m, SWA_KV_WIDTH), BF16),
                   jax.ShapeDtypeStruct((m, SWA_KV_WIDTH), BF16),
                   jax.ShapeDtypeStruct((batch, MLA_HEADS, seq, MLA_QK_PAD), BF16),
                   jax.ShapeDtypeStruct((batch, MLA_HEADS, seq, MLA_QK_PAD), BF16),
                   jax.ShapeDtypeStruct((batch, MLA_HEADS, MLA_V, seq), BF16)],
        compiler_params=pltpu.CompilerParams(
            dimension_semantics=("parallel", "parallel"), vmem_limit_bytes=VMEM_LIMIT),
        name="mix_in",
    )(x, gain, w_ext, cos4, sin4, qn, kvn, wa, wb, wk, wvt)


def _ssm_kernel(u_ref, p_ref, pt_ref, wb_ref, wc_ref, lam_ref, lamn_ref, d_ref, wglu_ref, bglu_ref,
                y_ref, s_ref, carry_ref, f_ref, *, t, lane_blk):
    n = t // SSM_STREAMS
    hs = SSM_HALF_STATES

    @pl.when(pl.program_id(1) == 0)
    def _():
        carry_ref[...] = jnp.zeros_like(carry_ref)

    u = u_ref[...]
    up = jnp.dot(p_ref[...], u.astype(BF16), preferred_element_type=F32).astype(BF16)
    for hf in range(2):
        s_ref[:, hf * 2 * hs:(hf + 1) * 2 * hs] = jnp.dot(
            up[:, hf * 256:(hf + 1) * 256], wb_ref[hf], preferred_element_type=F32)

    row_id = lax.broadcasted_iota(jnp.int32, (SSM_STREAMS, lane_blk), 0)
    for hf in range(2):
        for lb in range(hs // lane_blk):
            re0 = hf * 2 * hs + lb * lane_blk
            im0 = re0 + hs
            re_sl = slice(re0, re0 + lane_blk)
            im_sl = slice(im0, im0 + lane_blk)
            ar = jnp.broadcast_to(lam_ref[:, re_sl], (SSM_STREAMS, lane_blk))
            ai = jnp.broadcast_to(lam_ref[:, im_sl], (SSM_STREAMS, lane_blk))

            def scan(x0r, x0i, store):
                def body(i, c):
                    xr, xi = c
                    r0 = pl.multiple_of(i * SSM_STREAMS, SSM_STREAMS)
                    br = s_ref[pl.ds(r0, SSM_STREAMS), re_sl]
                    bi = s_ref[pl.ds(r0, SSM_STREAMS), im_sl]
                    nr = ar * xr - ai * xi + br
                    ni = ar * xi + ai * xr + bi
                    if store:
                        s_ref[pl.ds(r0, SSM_STREAMS), re_sl] = nr
                        s_ref[pl.ds(r0, SSM_STREAMS), im_sl] = ni
                    return nr, ni
                return lax.fori_loop(0, n, body, (x0r, x0i))

            zero = jnp.zeros((SSM_STREAMS, lane_blk), F32)
            fr, fi = scan(zero, zero, False)
            f_ref[:, 0:lane_blk] = fr
            f_ref[:, lane_blk:2 * lane_blk] = fi
            pr = lamn_ref[:, re_sl]
            pi = lamn_ref[:, im_sl]
            sr = carry_ref[:, re_sl]
            si = carry_ref[:, im_sl]
            s0r = jnp.zeros((SSM_STREAMS, lane_blk), F32)
            s0i = jnp.zeros((SSM_STREAMS, lane_blk), F32)
            for k in range(SSM_STREAMS):
                s0r = jnp.where(row_id == k, jnp.broadcast_to(sr, s0r.shape), s0r)
                s0i = jnp.where(row_id == k, jnp.broadcast_to(si, s0i.shape), s0i)
                er = f_ref[k:k + 1, 0:lane_blk]
                ei = f_ref[k:k + 1, lane_blk:2 * lane_blk]
                sr, si = pr * sr - pi * si + er, pr * si + pi * sr + ei
            carry_ref[:, re_sl] = sr
            carry_ref[:, im_sl] = si
            scan(s0r, s0i, True)

    yp = []
    for hf in range(2):
        xb = s_ref[:, hf * 2 * hs:(hf + 1) * 2 * hs].astype(BF16)
        yp.append(jnp.dot(xb, wc_ref[hf], preferred_element_type=F32))
    yp = jnp.concatenate(yp, axis=1)
    y_hi = yp.astype(BF16)
    y_lo = (yp - y_hi.astype(F32)).astype(BF16)
    pt = pt_ref[...]
    y = (jnp.dot(pt, y_hi, preferred_element_type=F32)
         + jnp.dot(pt, y_lo, preferred_element_type=F32))
    y = y + d_ref[...] * u
    c0 = math.sqrt(2.0 / math.pi)
    y = 0.5 * y * (1.0 + jnp.tanh(c0 * (y + 0.044715 * (y * y * y))))
    z = jnp.dot(y.astype(BF16), wglu_ref[...], preferred_element_type=F32) + bglu_ref[...]
    y_ref[...] = y * _sigmoid(z)


def _ssm(u, perm, perm_t, wb, wc, lam, lamn, d_skip, wglu, bglu, batch, seq):
    m = u.shape[0]
    t = min(SSM_T, seq)
    nt = seq // t
    row = lambda b, i: (b * nt + i, 0)
    kern = functools.partial(_ssm_kernel, t=t, lane_blk=512)
    return pl.pallas_call(
        kern,
        grid=(batch, nt),
        in_specs=[pl.BlockSpec((t, SSM_WIDTH), row),
                  _const_spec((t, t)), _const_spec((t, t)),
                  _const_spec((2, 256, 2 * SSM_HALF_STATES)),
                  _const_spec((2, 2 * SSM_HALF_STATES, 256)),
                  _const_spec((1, 4 * SSM_HALF_STATES)),
                  _const_spec((1, 4 * SSM_HALF_STATES)),
                  _const_spec((1, SSM_WIDTH)),
                  _const_spec((SSM_WIDTH, SSM_WIDTH)),
                  _const_spec((1, SSM_WIDTH))],
        out_specs=pl.BlockSpec((t, SSM_WIDTH), row),
        out_shape=jax.ShapeDtypeStruct((m, SSM_WIDTH), F32),
        scratch_shapes=[pltpu.VMEM((t, 4 * SSM_HALF_STATES), F32),
                        pltpu.VMEM((1, 4 * SSM_HALF_STATES), F32),
                        pltpu.VMEM((SSM_STREAMS, 1024), F32)],
        compiler_params=pltpu.CompilerParams(
            dimension_semantics=("parallel", "arbitrary"), vmem_limit_bytes=VMEM_LIMIT),
        name="ssm",
    )(u, perm, perm_t, wb, wc, lam, lamn, d_skip, wglu, bglu)


def _swa_kernel(sink_ref, q_ref, k_ref, v_ref, o_ref, *, tq):
    i = pl.program_id(1)
    tk = tq + SWA_WINDOW
    start = pl.multiple_of(jnp.maximum(i * tq - SWA_WINDOW, 0), SWA_WINDOW)
    kk = k_ref[pl.ds(start, tk), :]
    vv = v_ref[pl.ds(start, tk), :]
    qpos = i * tq + lax.broadcasted_iota(jnp.int32, (tq, tk), 0)
    kpos = start + lax.broadcasted_iota(jnp.int32, (tq, tk), 1)
    dist = qpos - kpos
    mask = (dist >= 0) & (dist < SWA_WINDOW)
    outs = []
    for hd in range(SWA_HEADS):
        g = hd // SWA_Q_PER_KV
        qh = q_ref[:, hd * SWA_HEAD_DIM:(hd + 1) * SWA_HEAD_DIM]
        kh = kk[:, g * SWA_HEAD_DIM:(g + 1) * SWA_HEAD_DIM]
        vh = vv[:, g * SWA_HEAD_DIM:(g + 1) * SWA_HEAD_DIM]
        s = lax.dot_general(qh, kh, (((1,), (1,)), ((), ())), preferred_element_type=F32)
        s = jnp.where(mask, s, -jnp.inf)
        sink = sink_ref[hd]
        mx = jnp.maximum(jnp.max(s, axis=-1, keepdims=True), sink)
        e = jnp.exp(s - mx)
        den = jnp.sum(e, axis=-1, keepdims=True) + jnp.exp(sink - mx)
        p = (e / den).astype(BF16)
        outs.append(jnp.dot(p, vh, preferred_element_type=F32))
    o_ref[...] = jnp.concatenate(outs, axis=1)


def _swa(sinks, q, k, v, batch, seq):
    m = q.shape[0]
    tq = min(SWA_TQ, seq)
    nt = seq // tq
    kern = functools.partial(_swa_kernel, tq=tq)
    k3 = k.reshape(batch, seq, SWA_KV_WIDTH)
    v3 = v.reshape(batch, seq, SWA_KV_WIDTH)
    return pl.pallas_call(
        kern,
        grid=(batch, nt),
        in_specs=[pl.BlockSpec(memory_space=pltpu.SMEM),
                  pl.BlockSpec((tq, SWA_WIDTH), lambda b, i: (b * nt + i, 0)),
                  pl.BlockSpec((None, seq, SWA_KV_WIDTH), lambda b, i: (b, 0, 0)),
                  pl.BlockSpec((None, seq, SWA_KV_WIDTH), lambda b, i: (b, 0, 0))],
        out_specs=pl.BlockSpec((tq, SWA_WIDTH), lambda b, i: (b * nt + i, 0)),
        out_shape=jax.ShapeDtypeStruct((m, SWA_WIDTH), F32),
        compiler_params=pltpu.CompilerParams(
            dimension_semantics=("parallel", "parallel"), vmem_limit_bytes=VMEM_LIMIT),
        name="swa",
    )(sinks, q, k3, v3)


def _mla_kernel(q_ref, k_ref, vt_ref, o_ref, m_ref, l_ref, acc_ref, *, tq, hp):
    qi = pl.program_id(2)
    m_ref[...] = jnp.full_like(m_ref, -jnp.inf)
    l_ref[...] = jnp.zeros_like(l_ref)
    acc_ref[...] = jnp.zeros_like(acc_ref)

    def block(j, masked):
        r0 = pl.multiple_of(j * tq, tq)
        for hh in range(hp):
            k = k_ref[hh, pl.ds(r0, tq), :]
            vt = vt_ref[hh, :, pl.ds(r0, tq)]
            st = lax.dot_general(k, q_ref[hh], (((1,), (1,)), ((), ())),
                                 preferred_element_type=F32)
            if masked:
                key = lax.broadcasted_iota(jnp.int32, (tq, tq), 0)
                qry = lax.broadcasted_iota(jnp.int32, (tq, tq), 1)
                st = jnp.where(key <= qry, st, -jnp.inf)
            m_old = m_ref[hh]
            m_new = jnp.maximum(m_old, jnp.max(st, axis=0, keepdims=True))
            alpha = jnp.exp2(m_old - m_new)
            p = jnp.exp2(st - m_new)
            l_ref[hh] = alpha * l_ref[hh] + jnp.sum(p, axis=0, keepdims=True)
            acc_ref[hh] = alpha * acc_ref[hh] + jnp.dot(vt, p.astype(BF16), preferred_element_type=F32)
            m_ref[hh] = m_new

    def body(j, c):
        block(j, False)
        return c

    lax.fori_loop(0, qi, body, 0)
    block(qi, True)
    for hh in range(hp):
        o_ref[:, hh * MLA_V:(hh + 1) * MLA_V] = (acc_ref[hh] / l_ref[hh]).T


def _mla(qc, kc, vt, batch, seq):
    tq = min(MLA_TQ, seq)
    nt = seq // tq
    hp = MLA_HEADS_PER_STEP
    kern = functools.partial(_mla_kernel, tq=tq, hp=hp)
    return pl.pallas_call(
        kern,
        grid=(batch, MLA_HEADS // hp, nt),
        in_specs=[pl.BlockSpec((None, hp, tq, MLA_QK_PAD), lambda b, h, i: (b, h, i, 0)),
                  pl.BlockSpec((None, hp, seq, MLA_QK_PAD), lambda b, h, i: (b, h, 0, 0)),
                  pl.BlockSpec((None, hp, MLA_V, seq), lambda b, h, i: (b, h, 0, 0))],
        out_specs=pl.BlockSpec((tq, hp * MLA_V), lambda b, h, i: (b * nt + i, h)),
        out_shape=jax.ShapeDtypeStruct((batch * seq, MLA_WIDTH), F32),
        scratch_shapes=[pltpu.VMEM((hp, 1, tq), F32), pltpu.VMEM((hp, 1, tq), F32),
                        pltpu.VMEM((hp, MLA_V, tq), F32)],
        compiler_params=pltpu.CompilerParams(
            dimension_semantics=("parallel", "parallel", "parallel"), vmem_limit_bytes=VMEM_LIMIT),
        name="mla",
    )(qc, kc, vt)


def _out_kernel(x_ref, ys_ref, yw_ref, ym_ref, g_ref, w_ref, o_ref):
    a = _rms(ys_ref[...], g_ref[:, 0:512]).astype(BF16)
    b = _rms(yw_ref[...], g_ref[:, 512:1024]).astype(BF16)
    c = _rms(ym_ref[...], g_ref[:, 1024:2048]).astype(BF16)
    acc = jnp.dot(a, w_ref[0:512, :], preferred_element_type=F32)
    acc += jnp.dot(b, w_ref[512:1024, :], preferred_element_type=F32)
    acc += jnp.dot(c, w_ref[1024:2048, :], preferred_element_type=F32)
    o_ref[...] = x_ref[...] + acc


def _out_proj(x, ys, yw, ym, gain, w):
    m = x.shape[0]
    tm = min(OUT_TM, m)
    row = lambda i: (i, 0)
    return pl.pallas_call(
        _out_kernel,
        grid=(m // tm,),
        in_specs=[pl.BlockSpec((tm, D_MODEL), row),
                  pl.BlockSpec((tm, SSM_WIDTH), row),
                  pl.BlockSpec((tm, SWA_WIDTH), row),
                  pl.BlockSpec((tm, MLA_WIDTH), row),
                  _const_spec((1, D_MODEL)),
                  _const_spec((D_MODEL, D_MODEL))],
        out_specs=pl.BlockSpec((tm, D_MODEL), row),
        out_shape=jax.ShapeDtypeStruct((m, D_MODEL), F32),
        compiler_params=pltpu.CompilerParams(
            dimension_semantics=("parallel",), vmem_limit_bytes=VMEM_LIMIT),
        name="out_proj",
    )(x, ys, yw, ym, gain, w)


def _final_kernel(x_ref, g_ref, o_ref):
    o_ref[...] = _rms(x_ref[...], g_ref[...])


def _final_norm(x, gain):
    m = x.shape[0]
    tm = min(1024, m)
    return pl.pallas_call(
        _final_kernel,
        grid=(m // tm,),
        in_specs=[pl.BlockSpec((tm, D_MODEL), lambda i: (i, 0)),
                  pl.BlockSpec((1, D_MODEL), lambda i: (0, 0))],
        out_specs=pl.BlockSpec((tm, D_MODEL), lambda i: (i, 0)),
        out_shape=jax.ShapeDtypeStruct((m, D_MODEL), F32),
        name="final_norm",
    )(x, gain)


def _swap_halves(n_heads, head_dim, base):
    half = head_dim // 2
    idx = np.arange(n_heads * head_dim).reshape(n_heads, head_dim)
    idx = np.concatenate([idx[:, half:], idx[:, :half]], axis=1).reshape(-1)
    return base + idx


def _ext_in_weight(w_in):
    o_qs, o_ks, o_vs, o_cq, o_ckv, o_kr = 512, 1024, 1152, 1280, 1792, 2048
    zeros64 = jnp.zeros((w_in.shape[0], 64), w_in.dtype)
    parts = [
        w_in[:, 0:512],
        w_in[:, o_qs:o_qs + 512],
        w_in[:, _swap_halves(SWA_HEADS, 64, o_qs)],
        w_in[:, o_ks:o_ks + 128],
        w_in[:, _swap_halves(2, 64, o_ks)],
        w_in[:, o_vs:o_vs + 128],
        w_in[:, o_cq:o_cq + 512],
        w_in[:, o_ckv:o_ckv + 256],
        w_in[:, o_kr:o_kr + 64], zeros64,
        w_in[:, _swap_halves(1, 64, o_kr)], zeros64,
    ]
    return jnp.concatenate(parts, axis=1).astype(BF16)


def _mla_q_weights(w_uq):
    w = w_uq.reshape(MLA_Q_RANK, MLA_HEADS, MLA_NOPE + MLA_ROPE)
    rope = w[:, :, MLA_NOPE:]
    z = jnp.zeros((MLA_Q_RANK, MLA_HEADS, 64), w.dtype)
    wa = jnp.concatenate([w, z], axis=2).reshape(MLA_Q_RANK, MLA_HEADS * 256)
    swapped = jnp.concatenate([rope[:, :, 32:], rope[:, :, :32]], axis=2)
    wb = jnp.concatenate([swapped, z], axis=2).reshape(MLA_Q_RANK, MLA_HEADS * LANES)
    return wa.astype(BF16), wb.astype(BF16)


def _mla_kv_weights(w_ukv):
    w = w_ukv.reshape(MLA_KV_RANK, MLA_HEADS, MLA_NOPE + MLA_V)
    wk = w[:, :, :MLA_NOPE].reshape(MLA_KV_RANK, MLA_HEADS * MLA_NOPE)
    wvt = w[:, :, MLA_NOPE:].reshape(MLA_KV_RANK, MLA_HEADS * MLA_V).T
    return wk.astype(BF16), wvt.astype(BF16)


def _ssm_params(log_dt, a_re, a_im, b_re, b_im, c_re, c_im, n_sub):
    lr, li = a_re.astype(F32), a_im.astype(F32)
    dt = jnp.exp(log_dt.astype(F32))[:, None]
    mag = jnp.exp(lr * dt)
    abar_r = mag * jnp.cos(li * dt)
    abar_i = mag * jnp.sin(li * dt)
    den = lr * lr + li * li
    nr = abar_r - 1.0
    qr = (nr * lr + abar_i * li) / den
    qi = (abar_i * lr - nr * li) / den
    br, bi = b_re.astype(F32), b_im.astype(F32)
    bbar_r = qr[..., None] * br - qi[..., None] * bi
    bbar_i = qr[..., None] * bi + qi[..., None] * br
    pr, pi = abar_r, abar_i
    for _ in range(int(round(math.log2(n_sub)))):
        pr, pi = pr * pr - pi * pi, 2.0 * pr * pi

    hg = SSM_HALF_GROUPS
    eye = jnp.eye(hg, dtype=F32)

    def lanes(re, im):
        re = re.reshape(2, SSM_HALF_STATES)
        im = im.reshape(2, SSM_HALF_STATES)
        return jnp.concatenate([re, im], axis=1).reshape(1, 4 * SSM_HALF_STATES)

    def b_block(bb):
        bb = bb.reshape(2, hg, SSM_STATE, SSM_GROUP)
        w = jnp.einsum('hgpc,gk->hgckp', bb, eye)
        return w.reshape(2, hg * SSM_GROUP, SSM_HALF_STATES)

    def c_block(cc):
        cc = cc.reshape(2, hg, SSM_GROUP, SSM_STATE)
        w = jnp.einsum('hgcp,gk->hgpkc', cc, eye)
        return w.reshape(2, SSM_HALF_STATES, hg * SSM_GROUP)

    wb = jnp.concatenate([b_block(bbar_r), b_block(bbar_i)], axis=2).astype(BF16)
    wc = jnp.concatenate([c_block(c_re.astype(F32)), -c_block(c_im.astype(F32))], axis=1).astype(BF16)
    return wb, wc, lanes(abar_r, abar_i), lanes(pr, pi)


def _perm_matrices(t):
    n = t // SSM_STREAMS
    r = np.arange(t)
    tok = (r % SSM_STREAMS) * n + r // SSM_STREAMS
    p = np.zeros((t, t), np.float32)
    p[r, tok] = 1.0
    return jnp.asarray(p, BF16), jnp.asarray(p.T, BF16)


def kernel(x, positions, ffn1_norm, ffn1_w_gate, ffn1_w_up, ffn1_w_down, mix_norm, w_in, ssm_log_dt, ssm_a_re, ssm_a_im, ssm_b_re, ssm_b_im, ssm_c_re, ssm_c_im, ssm_d, ssm_w_glu, ssm_b_glu, swa_sinks, mla_q_norm, mla_w_uq, mla_kv_norm, mla_w_ukv, out_norm, w_out, ffn2_norm, ffn2_w_gate, ffn2_w_up, ffn2_w_down, final_norm):
    batch, seq, _ = x.shape
    depth = w_in.shape[0]
    m = batch * seq
    xf = x.reshape(m, D_MODEL)
    cos4, sin4 = _rope_tables(positions)
    t_ssm = min(SSM_T, seq)
    perm, perm_t = _perm_matrices(t_ssm)

    for l in range(depth):
        xf = _ffn(xf, ffn1_norm[:, None, :], ffn1_w_gate, ffn1_w_up, ffn1_w_down, l)
        wa, wb_q = _mla_q_weights(mla_w_uq[l])
        wk, wvt = _mla_kv_weights(mla_w_ukv[l])
        u, qs, ks, vs, qc, kc, vm = _mix_in(
            xf, mix_norm[l][None, :], _ext_in_weight(w_in[l]), cos4, sin4,
            mla_q_norm[l][None, :], mla_kv_norm[l][None, :], wa, wb_q, wk, wvt, batch, seq)
        wb, wc, lam, lamn = _ssm_params(ssm_log_dt[l], ssm_a_re[l], ssm_a_im[l], ssm_b_re[l],
                                        ssm_b_im[l], ssm_c_re[l], ssm_c_im[l], t_ssm // SSM_STREAMS)
        y_ssm = _ssm(u, perm, perm_t, wb, wc, lam, lamn, ssm_d[l][None, :],
                     ssm_w_glu[l].astype(BF16), ssm_b_glu[l][None, :], batch, seq)
        y_swa = _swa(swa_sinks[l], qs, ks, vs, batch, seq)
        y_mla = _mla(qc, kc, vm, batch, seq)
        xf = _out_proj(xf, y_ssm, y_swa, y_mla, out_norm[l][None, :], w_out[l].astype(BF16))
        xf = _ffn(xf, ffn2_norm[:, None, :], ffn2_w_gate, ffn2_w_up, ffn2_w_down, l)
    return _final_norm(xf, final_norm[None, :]).reshape(batch, seq, D_MODEL)
```

```python
import functools
import math

import jax
import jax.numpy as jnp
import numpy as np
from jax import lax
from jax.experimental import pallas as pl
from jax.experimental.pallas import tpu as pltpu

F32 = jnp.float32
BF16 = jnp.bfloat16

D_MODEL = 2048
D_FF = 5632
EPS = 1e-6
ROPE_THETA = 10000.0

SSM_GROUP = 16
SSM_WIDTH = 512
SSM_GROUPS = 32
SSM_STATE = 64
SSM_HALF_GROUPS = 16
SSM_HALF_STATES = SSM_HALF_GROUPS * SSM_STATE
SSM_STREAMS = 8

SWA_HEADS = 8
SWA_Q_PER_KV = 4
SWA_KV_HEADS = 2
SWA_HEAD_DIM = 64
SWA_WINDOW = 128
SWA_WIDTH = 512
SWA_KV_WIDTH = 128

MLA_HEADS = 8
MLA_Q_RANK = 512
MLA_KV_RANK = 256
MLA_NOPE = 128
MLA_ROPE = 64
MLA_V = 128
MLA_WIDTH = 1024
MLA_QK_PAD = 256
ROPE_DIM = 64
LANES = 128
LOG2E = 1.4426950408889634

VMEM_LIMIT = 56 * 1024 * 1024

FFN_TM = 1024
FFN_TF = 256
MIX_TM = 512
SSM_T = 512
SWA_TQ = 512
MLA_TQ = 512
MLA_TK = 512
MLA_HEADS_PER_STEP = 4
OUT_TM = 512
ROPE_TM = 2048

_C_U = 0
_C_QS = 512
_C_QS_ROT = 1024
_C_KS = 1536
_C_KS_ROT = 1664
_C_KR = 1792
_C_KR_ROT = 1920
_C_VS = 2048
_C_CQ = 2176
_C_CKV = 2688
N_EXT = 2944


def _rms(x, gain):
    ms = jnp.mean(x * x, axis=-1, keepdims=True)
    return x * lax.rsqrt(ms + EPS) * gain


def _sigmoid(x):
    return 1.0 / (1.0 + jnp.exp(-x))


def _const_spec(shape):
    nd = len(shape)
    return pl.BlockSpec(shape, lambda *_: (0,) * nd, pipeline_mode=pl.Buffered(1))


def _rope_kernel(pos_ref, freq_ref, sign_ref, cos_ref, sin_ref):
    ang = pos_ref[...].astype(F32) * freq_ref[...]
    cos_ref[...] = jnp.cos(ang)
    sin_ref[...] = jnp.sin(ang) * sign_ref[...]


def _rope_tables(positions):
    m = positions.size
    tm = min(ROPE_TM, m)
    inv_freq = ROPE_THETA ** (-jnp.arange(0, ROPE_DIM, 2, dtype=F32) / ROPE_DIM)
    freq4 = jnp.tile(inv_freq, 4)[None, :]
    sign4 = jnp.tile(jnp.concatenate([-jnp.ones(32, F32), jnp.ones(32, F32)]), 2)[None, :]
    return pl.pallas_call(
        _rope_kernel,
        grid=(m // tm,),
        in_specs=[pl.BlockSpec((tm, 1), lambda i: (i, 0)),
                  pl.BlockSpec((1, LANES), lambda i: (0, 0)),
                  pl.BlockSpec((1, LANES), lambda i: (0, 0))],
        out_specs=[pl.BlockSpec((tm, LANES), lambda i: (i, 0))] * 2,
        out_shape=[jax.ShapeDtypeStruct((m, LANES), F32)] * 2,
        name="rope_tables",
    )(positions.reshape(m, 1), freq4, sign4)


def _ffn_kernel(x_ref, g_ref, wg_ref, wu_ref, wd_ref, o_ref, h_ref):
    f = pl.program_id(1)

    @pl.when(f == 0)
    def _():
        x = x_ref[...]
        h_ref[...] = _rms(x, g_ref[...]).astype(BF16)
        o_ref[...] = x

    h = h_ref[...]
    g = jnp.dot(h, wg_ref[...].astype(BF16), preferred_element_type=F32)
    u = jnp.dot(h, wu_ref[...].astype(BF16), preferred_element_type=F32)
    a = (g * _sigmoid(g) * (0.5 * u)).astype(BF16)
    o_ref[...] += jnp.dot(a, wd_ref[...].astype(BF16), preferred_element_type=F32)


def _ffn(x, gain, wg, wu, wd, layer):
    m = x.shape[0]
    tm = min(FFN_TM, m)
    return pl.pallas_call(
        _ffn_kernel,
        grid=(m // tm, D_FF // FFN_TF),
        in_specs=[pl.BlockSpec((tm, D_MODEL), lambda i, f: (i, 0)),
                  pl.BlockSpec((None, 1, D_MODEL), lambda i, f: (layer, 0, 0)),
                  pl.BlockSpec((None, D_MODEL, FFN_TF), lambda i, f: (layer, 0, f)),
                  pl.BlockSpec((None, D_MODEL, FFN_TF), lambda i, f: (layer, 0, f)),
                  pl.BlockSpec((None, FFN_TF, D_MODEL), lambda i, f: (layer, f, 0))],
        out_specs=pl.BlockSpec((tm, D_MODEL), lambda i, f: (i, 0)),
        out_shape=jax.ShapeDtypeStruct((m, D_MODEL), F32),
        scratch_shapes=[pltpu.VMEM((tm, D_MODEL), BF16)],
        compiler_params=pltpu.CompilerParams(
            dimension_semantics=("parallel", "arbitrary"), vmem_limit_bytes=VMEM_LIMIT),
        name="ffn",
    )(x, gain, wg, wu, wd)


def _mix_in_kernel(x_ref, g_ref, w_ref, cos_ref, sin_ref, qn_ref, kvn_ref, wa_ref, wb_ref, wk_ref,
                   wvt_ref, u_ref, qs_ref, ks_ref, vst_ref, qc_ref, kc_ref, vt_ref):
    h = _rms(x_ref[...], g_ref[...]).astype(BF16)

    def seg(lo, n):
        return jnp.dot(h, w_ref[:, lo:lo + n], preferred_element_type=F32)

    cos4 = cos_ref[...]
    sin4 = sin_ref[...]
    cos512 = jnp.concatenate([cos4] * 4, axis=1)
    sin512 = jnp.concatenate([sin4] * 4, axis=1)

    u_ref[...] = seg(_C_U, SSM_WIDTH)
    qs = seg(_C_QS, SWA_WIDTH) * cos512 + seg(_C_QS_ROT, SWA_WIDTH) * sin512
    qs = (qs * (SWA_HEAD_DIM ** -0.5 * LOG2E)).astype(BF16)
    for hd in range(SWA_HEADS):
        qs_ref[hd] = qs[:, hd * SWA_HEAD_DIM:(hd + 1) * SWA_HEAD_DIM]
    small = seg(_C_KS, 5 * LANES)
    ks = (small[:, 0:LANES] * cos4 + small[:, LANES:2 * LANES] * sin4).astype(BF16)
    for g in range(SWA_KV_HEADS):
        ks_ref[g] = ks[:, g * SWA_HEAD_DIM:(g + 1) * SWA_HEAD_DIM]
    kr = (small[:, 2 * LANES:3 * LANES] * cos4 + small[:, 3 * LANES:4 * LANES] * sin4).astype(BF16)
    vst_ref[...] = small[:, 4 * LANES:5 * LANES].T.astype(BF16)

    cqn = _rms(seg(_C_CQ, MLA_Q_RANK), qn_ref[...]).astype(BF16)
    ckvn = _rms(seg(_C_CKV, MLA_KV_RANK), kvn_ref[...]).astype(BF16)

    scale = (MLA_NOPE + MLA_ROPE) ** -0.5 * LOG2E
    qa = jnp.dot(cqn, wa_ref[...], preferred_element_type=F32)
    qb = jnp.dot(cqn, wb_ref[...], preferred_element_type=F32)
    kn = jnp.dot(ckvn, wk_ref[...], preferred_element_type=F32)
    vt = lax.dot_general(wvt_ref[...], ckvn, (((1,), (1,)), ((), ())), preferred_element_type=F32)
    for hd in range(MLA_HEADS):
        q_nope = qa[:, hd * 256:hd * 256 + LANES]
        q_rope = qa[:, hd * 256 + LANES:(hd + 1) * 256]
        q_swap = qb[:, hd * LANES:(hd + 1) * LANES]
        qc_ref[hd, :, 0:LANES] = (q_nope * scale).astype(BF16)
        qc_ref[hd, :, LANES:2 * LANES] = ((q_rope * cos4 + q_swap * sin4) * scale).astype(BF16)
        kc_ref[hd, :, 0:LANES] = kn[:, hd * LANES:(hd + 1) * LANES].astype(BF16)
        kc_ref[hd, :, LANES:2 * LANES] = kr
        vt_ref[hd] = vt[hd * MLA_V:(hd + 1) * MLA_V, :].astype(BF16)


def _mix_in(x, gain, w_ext, cos4, sin4, qn, kvn, wa, wb, wk, wvt, batch, seq):
    m = x.shape[0]
    tm = min(MIX_TM, seq)
    nt = seq // tm
    row = lambda b, i: (b * nt + i, 0)
    head_spec = lambda n, w: pl.BlockSpec((None, n, tm, w), lambda b, i: (b, 0, i, 0))
    return pl.pallas_call(
        _mix_in_kernel,
        grid=(batch, nt),
        in_specs=[pl.BlockSpec((tm, D_MODEL), row),
                  _const_spec((1, D_MODEL)),
                  _const_spec((D_MODEL, N_EXT)),
                  pl.BlockSpec((tm, LANES), row),
                  pl.BlockSpec((tm, LANES), row),
                  _const_spec((1, MLA_Q_RANK)),
                  _const_spec((1, MLA_KV_RANK)),
                  _const_spec((MLA_Q_RANK, MLA_HEADS * 256)),
                  _const_spec((MLA_Q_RANK, MLA_HEADS * LANES)),
                  _const_spec((MLA_KV_RANK, MLA_HEADS * MLA_NOPE)),
                  _const_spec((MLA_HEADS * MLA_V, MLA_KV_RANK))],
        out_specs=[pl.BlockSpec((tm, SSM_WIDTH), row),
                   head_spec(SWA_HEADS, SWA_HEAD_DIM),
                   head_spec(SWA_KV_HEADS, SWA_HEAD_DIM),
                   pl.BlockSpec((None, SWA_KV_WIDTH, tm), lambda b, i: (b, 0, i)),
                   head_spec(MLA_HEADS, MLA_QK_PAD), head_spec(MLA_HEADS, MLA_QK_PAD),
                   pl.BlockSpec((None, MLA_HEADS, MLA_V, tm), lambda b, i: (b, 0, 0, i))],
        out_shape=[jax.ShapeDtypeStruct((m, SSM_WIDTH), F32),
                   jax.ShapeDtypeStruct((batch, SWA_HEADS, seq, SWA_HEAD_DIM), BF16),
                   jax.ShapeDtypeStruct((batch, SWA_KV_HEADS, seq, SWA_HEAD_DIM), BF16),
                   jax.ShapeDtypeStruct((batch, SWA_KV_WIDTH, seq), BF16),
                   jax.ShapeDtypeStruct((batch, MLA_HEADS, seq, MLA_QK_PAD), BF16),
                   jax.ShapeDtypeStruct((batch, MLA_HEADS, seq, MLA_QK_PAD), BF16),
                   jax.ShapeDtypeStruct((batch, MLA_HEADS, MLA_V, seq), BF16)],
        compiler_params=pltpu.CompilerParams(
            dimension_semantics=("parallel", "parallel"), vmem_limit_bytes=VMEM_LIMIT),
        name="mix_in",
    )(x, gain, w_ext, cos4, sin4, qn, kvn, wa, wb, wk, wvt)


def _ssm_kernel(u_ref, p_ref, pt_ref, wb_ref, wc_ref, lam_ref, lamn_ref, d_ref, wglu_ref, bglu_ref,
                y_ref, s_ref, carry_ref, f_ref, *, t, lane_blk):
    n = t // SSM_STREAMS
    hs = SSM_HALF_STATES

    @pl.when(pl.program_id(1) == 0)
    def _():
        carry_ref[...] = jnp.zeros_like(carry_ref)

    u = u_ref[...]
    up = jnp.dot(p_ref[...], u.astype(BF16), preferred_element_type=F32).astype(BF16)
    for hf in range(2):
        s_ref[:, hf * 2 * hs:(hf + 1) * 2 * hs] = jnp.dot(
            up[:, hf * 256:(hf + 1) * 256], wb_ref[hf], preferred_element_type=F32)

    row_id = lax.broadcasted_iota(jnp.int32, (SSM_STREAMS, lane_blk), 0)
    for hf in range(2):
        for lb in range(hs // lane_blk):
            re0 = hf * 2 * hs + lb * lane_blk
            im0 = re0 + hs
            re_sl = slice(re0, re0 + lane_blk)
            im_sl = slice(im0, im0 + lane_blk)
            ar = jnp.broadcast_to(lam_ref[:, re_sl], (SSM_STREAMS, lane_blk))
            ai = jnp.broadcast_to(lam_ref[:, im_sl], (SSM_STREAMS, lane_blk))

            def scan(x0r, x0i, store):
                def body(i, c):
                    xr, xi = c
                    r0 = pl.multiple_of(i * SSM_STREAMS, SSM_STREAMS)
                    br = s_ref[pl.ds(r0, SSM_STREAMS), re_sl]
                    bi = s_ref[pl.ds(r0, SSM_STREAMS), im_sl]
                    nr = ar * xr - ai * xi + br
                    ni = ar * xi + ai * xr + bi
                    if store:
                        s_ref[pl.ds(r0, SSM_STREAMS), re_sl] = nr
                        s_ref[pl.ds(r0, SSM_STREAMS), im_sl] = ni
                    return nr, ni
                return lax.fori_loop(0, n, body, (x0r, x0i), unroll=True)

            zero = jnp.zeros((SSM_STREAMS, lane_blk), F32)
            fr, fi = scan(zero, zero, False)
            f_ref[:, 0:lane_blk] = fr
            f_ref[:, lane_blk:2 * lane_blk] = fi
            pr = lamn_ref[:, re_sl]
            pi = lamn_ref[:, im_sl]
            sr = carry_ref[:, re_sl]
            si = carry_ref[:, im_sl]
            s0r = jnp.zeros((SSM_STREAMS, lane_blk), F32)
            s0i = jnp.zeros((SSM_STREAMS, lane_blk), F32)
            for k in range(SSM_STREAMS):
                s0r = jnp.where(row_id == k, jnp.broadcast_to(sr, s0r.shape), s0r)
                s0i = jnp.where(row_id == k, jnp.broadcast_to(si, s0i.shape), s0i)
                er = f_ref[k:k + 1, 0:lane_blk]
                ei = f_ref[k:k + 1, lane_blk:2 * lane_blk]
                sr, si = pr * sr - pi * si + er, pr * si + pi * sr + ei
            carry_ref[:, re_sl] = sr
            carry_ref[:, im_sl] = si
            scan(s0r, s0i, True)

    yp = []
    for hf in range(2):
        xb = s_ref[:, hf * 2 * hs:(hf + 1) * 2 * hs].astype(BF16)
        yp.append(jnp.dot(xb, wc_ref[hf], preferred_element_type=F32))
    yp = jnp.concatenate(yp, axis=1)
    y_hi = yp.astype(BF16)
    y_lo = (yp - y_hi.astype(F32)).astype(BF16)
    pt = pt_ref[...]
    y = (jnp.dot(pt, y_hi, preferred_element_type=F32)
         + jnp.dot(pt, y_lo, preferred_element_type=F32))
    y = y + d_ref[...] * u
    c0 = math.sqrt(2.0 / math.pi)
    y = 0.5 * y * (1.0 + jnp.tanh(c0 * (y + 0.044715 * (y * y * y))))
    z = jnp.dot(y.astype(BF16), wglu_ref[...], preferred_element_type=F32) + bglu_ref[...]
    y_ref[...] = y * _sigmoid(z)


def _ssm(u, perm, perm_t, wb, wc, lam, lamn, d_skip, wglu, bglu, batch, seq):
    m = u.shape[0]
    t = min(SSM_T, seq)
    nt = seq // t
    row = lambda b, i: (b * nt + i, 0)
    kern = functools.partial(_ssm_kernel, t=t, lane_blk=512)
    return pl.pallas_call(
        kern,
        grid=(batch, nt),
        in_specs=[pl.BlockSpec((t, SSM_WIDTH), row),
                  _const_spec((t, t)), _const_spec((t, t)),
                  _const_spec((2, 256, 2 * SSM_HALF_STATES)),
                  _const_spec((2, 2 * SSM_HALF_STATES, 256)),
                  _const_spec((1, 4 * SSM_HALF_STATES)),
                  _const_spec((1, 4 * SSM_HALF_STATES)),
                  _const_spec((1, SSM_WIDTH)),
                  _const_spec((SSM_WIDTH, SSM_WIDTH)),
                  _const_spec((1, SSM_WIDTH))],
        out_specs=pl.BlockSpec((t, SSM_WIDTH), row),
        out_shape=jax.ShapeDtypeStruct((m, SSM_WIDTH), F32),
        scratch_shapes=[pltpu.VMEM((t, 4 * SSM_HALF_STATES), F32),
                        pltpu.VMEM((1, 4 * SSM_HALF_STATES), F32),
                        pltpu.VMEM((SSM_STREAMS, 1024), F32)],
        compiler_params=pltpu.CompilerParams(
            dimension_semantics=("parallel", "arbitrary"), vmem_limit_bytes=VMEM_LIMIT),
        name="ssm",
    )(u, perm, perm_t, wb, wc, lam, lamn, d_skip, wglu, bglu)


def _swa_kernel(sink_ref, q_ref, k_ref, vt_ref, o_ref, ot_ref, *, tq):
    i = pl.program_id(1)
    blk = SWA_WINDOW
    nk = 2 * blk
    nq = SWA_Q_PER_KV * blk
    lane = lax.broadcasted_iota(jnp.int32, (1, nq), 1)
    key_row = lax.broadcasted_iota(jnp.int32, (nk, nq), 0)
    qry_col = lax.broadcasted_iota(jnp.int32, (nk, nq), 1) & (blk - 1)
    rel = qry_col - key_row
    for g in range(SWA_KV_HEADS):
        sink_row = jnp.zeros((1, nq), F32)
        for hh in range(SWA_Q_PER_KV):
            sink_row = jnp.where(lane // blk == hh, sink_ref[g * SWA_Q_PER_KV + hh] * LOG2E, sink_row)
        for qb in range(tq // blk):
            qstart = i * tq + qb * blk
            kstart = pl.multiple_of(jnp.maximum(qstart - blk, 0), blk)
            kk = k_ref[g, pl.ds(kstart, nk), :]
            vt = vt_ref[g * SWA_HEAD_DIM:(g + 1) * SWA_HEAD_DIM, pl.ds(kstart, nk)]
            qst = q_ref[g * SWA_Q_PER_KV:(g + 1) * SWA_Q_PER_KV, qb * blk:(qb + 1) * blk, :]
            qst = qst.reshape(nq, SWA_HEAD_DIM)
            st = lax.dot_general(kk, qst, (((1,), (1,)), ((), ())), preferred_element_type=F32)
            dist = rel + (qstart - kstart)
            st = jnp.where((dist >= 0) & (dist < SWA_WINDOW), st, -jnp.inf)
            mx = jnp.maximum(jnp.max(st, axis=0, keepdims=True), sink_row)
            e = jnp.exp2(st - mx)
            den = jnp.sum(e, axis=0, keepdims=True) + jnp.exp2(sink_row - mx)
            ot = jnp.dot(vt, e.astype(BF16), preferred_element_type=F32) / den
            for hh in range(SWA_Q_PER_KV):
                hd = g * SWA_Q_PER_KV + hh
                ot_ref[hd * SWA_HEAD_DIM:(hd + 1) * SWA_HEAD_DIM, qb * blk:(qb + 1) * blk] = (
                    ot[:, hh * blk:(hh + 1) * blk])
    o_ref[...] = ot_ref[...].T


def _swa(sinks, q, k, vt, batch, seq):
    tq = min(SWA_TQ, seq)
    nt = seq // tq
    kern = functools.partial(_swa_kernel, tq=tq)
    return pl.pallas_call(
        kern,
        grid=(batch, nt),
        in_specs=[pl.BlockSpec(memory_space=pltpu.SMEM),
                  pl.BlockSpec((None, SWA_HEADS, tq, SWA_HEAD_DIM), lambda b, i: (b, 0, i, 0)),
                  pl.BlockSpec((None, SWA_KV_HEADS, seq, SWA_HEAD_DIM), lambda b, i: (b, 0, 0, 0)),
                  pl.BlockSpec((None, SWA_KV_WIDTH, seq), lambda b, i: (b, 0, 0))],
        out_specs=pl.BlockSpec((tq, SWA_WIDTH), lambda b, i: (b * nt + i, 0)),
        out_shape=jax.ShapeDtypeStruct((batch * seq, SWA_WIDTH), F32),
        scratch_shapes=[pltpu.VMEM((SWA_WIDTH, tq), F32)],
        compiler_params=pltpu.CompilerParams(
            dimension_semantics=("parallel", "parallel"), vmem_limit_bytes=VMEM_LIMIT),
        name="swa",
    )(sinks, q, k, vt)


def _mla_kernel(q_ref, k_ref, vt_ref, o_ref, m_ref, l_ref, acc_ref, *, tq, tk, hp):
    qi = pl.program_id(2)
    m_ref[...] = jnp.full_like(m_ref, -jnp.inf)
    l_ref[...] = jnp.zeros_like(l_ref)
    acc_ref[...] = jnp.zeros_like(acc_ref)
    sub = tq // tk

    def block(r0, diag):
        sts = []
        for hh in range(hp):
            k = k_ref[hh, pl.ds(r0, tk), :]
            st = lax.dot_general(k, q_ref[hh], (((1,), (1,)), ((), ())),
                                 preferred_element_type=F32)
            if diag is not None:
                key = lax.broadcasted_iota(jnp.int32, (tk, tq), 0) + diag * tk
                qry = lax.broadcasted_iota(jnp.int32, (tk, tq), 1)
                st = jnp.where(key <= qry, st, -jnp.inf)
            sts.append(st)
        for hh in range(hp):
            st = sts[hh]
            vt = vt_ref[hh, :, pl.ds(r0, tk)]
            m_old = m_ref[hh]
            m_new = jnp.maximum(m_old, jnp.max(st, axis=0, keepdims=True))
            alpha = jnp.exp2(m_old - m_new)
            p = jnp.exp2(st - m_new)
            l_ref[hh] = alpha * l_ref[hh] + jnp.sum(p, axis=0, keepdims=True)
            acc_ref[hh] = alpha * acc_ref[hh] + jnp.dot(vt, p.astype(BF16), preferred_element_type=F32)
            m_ref[hh] = m_new

    def body(j, c):
        block(pl.multiple_of(j * tk, tk), None)
        return c

    lax.fori_loop(0, qi * sub, body, 0)
    for d in range(sub):
        block(pl.multiple_of(qi * tq + d * tk, tk), d)
    for hh in range(hp):
        o_ref[:, hh * MLA_V:(hh + 1) * MLA_V] = (acc_ref[hh] / l_ref[hh]).T


def _mla(qc, kc, vt, batch, seq):
    tq = min(MLA_TQ, seq)
    tk = min(MLA_TK, tq)
    nt = seq // tq
    hp = MLA_HEADS_PER_STEP
    kern = functools.partial(_mla_kernel, tq=tq, tk=tk, hp=hp)
    return pl.pallas_call(
        kern,
        grid=(batch, MLA_HEADS // hp, nt),
        in_specs=[pl.BlockSpec((None, hp, tq, MLA_QK_PAD), lambda b, h, i: (b, h, i, 0)),
                  pl.BlockSpec((None, hp, seq, MLA_QK_PAD), lambda b, h, i: (b, h, 0, 0)),
                  pl.BlockSpec((None, hp, MLA_V, seq), lambda b, h, i: (b, h, 0, 0))],
        out_specs=pl.BlockSpec((tq, hp * MLA_V), lambda b, h, i: (b * nt + i, h)),
        out_shape=jax.ShapeDtypeStruct((batch * seq, MLA_WIDTH), F32),
        scratch_shapes=[pltpu.VMEM((hp, 1, tq), F32), pltpu.VMEM((hp, 1, tq), F32),
                        pltpu.VMEM((hp, MLA_V, tq), F32)],
        compiler_params=pltpu.CompilerParams(
            dimension_semantics=("parallel", "parallel", "parallel"), vmem_limit_bytes=VMEM_LIMIT),
        name="mla",
    )(qc, kc, vt)


def _out_kernel(x_ref, ys_ref, yw_ref, ym_ref, g_ref, w_ref, o_ref):
    a = _rms(ys_ref[...], g_ref[:, 0:512]).astype(BF16)
    b = _rms(yw_ref[...], g_ref[:, 512:1024]).astype(BF16)
    c = _rms(ym_ref[...], g_ref[:, 1024:2048]).astype(BF16)
    acc = jnp.dot(a, w_ref[0:512, :], preferred_element_type=F32)
    acc += jnp.dot(b, w_ref[512:1024, :], preferred_element_type=F32)
    acc += jnp.dot(c, w_ref[1024:2048, :], preferred_element_type=F32)
    o_ref[...] = x_ref[...] + acc


def _out_proj(x, ys, yw, ym, gain, w):
    m = x.shape[0]
    tm = min(OUT_TM, m)
    row = lambda i: (i, 0)
    return pl.pallas_call(
        _out_kernel,
        grid=(m // tm,),
        in_specs=[pl.BlockSpec((tm, D_MODEL), row),
                  pl.BlockSpec((tm, SSM_WIDTH), row),
                  pl.BlockSpec((tm, SWA_WIDTH), row),
                  pl.BlockSpec((tm, MLA_WIDTH), row),
                  _const_spec((1, D_MODEL)),
                  _const_spec((D_MODEL, D_MODEL))],
        out_specs=pl.BlockSpec((tm, D_MODEL), row),
        out_shape=jax.ShapeDtypeStruct((m, D_MODEL), F32),
        compiler_params=pltpu.CompilerParams(
            dimension_semantics=("parallel",), vmem_limit_bytes=VMEM_LIMIT),
        name="out_proj",
    )(x, ys, yw, ym, gain, w)


def _final_kernel(x_ref, g_ref, o_ref):
    o_ref[...] = _rms(x_ref[...], g_ref[...])


def _final_norm(x, gain):
    m = x.shape[0]
    tm = min(1024, m)
    return pl.pallas_call(
        _final_kernel,
        grid=(m // tm,),
        in_specs=[pl.BlockSpec((tm, D_MODEL), lambda i: (i, 0)),
                  pl.BlockSpec((1, D_MODEL), lambda i: (0, 0))],
        out_specs=pl.BlockSpec((tm, D_MODEL), lambda i: (i, 0)),
        out_shape=jax.ShapeDtypeStruct((m, D_MODEL), F32),
        name="final_norm",
    )(x, gain)


def _swap_halves(n_heads, head_dim, base):
    half = head_dim // 2
    idx = np.arange(n_heads * head_dim).reshape(n_heads, head_dim)
    idx = np.concatenate([idx[:, half:], idx[:, :half]], axis=1).reshape(-1)
    return base + idx


def _ext_in_weight(w_in):
    o_qs, o_ks, o_vs, o_cq, o_ckv, o_kr = 512, 1024, 1152, 1280, 1792, 2048
    zeros64 = jnp.zeros((w_in.shape[0], 64), w_in.dtype)
    parts = [
        w_in[:, 0:512],
        w_in[:, o_qs:o_qs + 512],
        w_in[:, _swap_halves(SWA_HEADS, 64, o_qs)],
        w_in[:, o_ks:o_ks + 128],
        w_in[:, _swap_halves(2, 64, o_ks)],
        w_in[:, o_kr:o_kr + 64], zeros64,
        w_in[:, _swap_halves(1, 64, o_kr)], zeros64,
        w_in[:, o_vs:o_vs + 128],
        w_in[:, o_cq:o_cq + 512],
        w_in[:, o_ckv:o_ckv + 256],
    ]
    return jnp.concatenate(parts, axis=1).astype(BF16)


def _mla_q_weights(w_uq):
    w = w_uq.reshape(MLA_Q_RANK, MLA_HEADS, MLA_NOPE + MLA_ROPE)
    rope = w[:, :, MLA_NOPE:]
    z = jnp.zeros((MLA_Q_RANK, MLA_HEADS, 64), w.dtype)
    wa = jnp.concatenate([w, z], axis=2).reshape(MLA_Q_RANK, MLA_HEADS * 256)
    swapped = jnp.concatenate([rope[:, :, 32:], rope[:, :, :32]], axis=2)
    wb = jnp.concatenate([swapped, z], axis=2).reshape(MLA_Q_RANK, MLA_HEADS * LANES)
    return wa.astype(BF16), wb.astype(BF16)


def _mla_kv_weights(w_ukv):
    w = w_ukv.reshape(MLA_KV_RANK, MLA_HEADS, MLA_NOPE + MLA_V)
    wk = w[:, :, :MLA_NOPE].reshape(MLA_KV_RANK, MLA_HEADS * MLA_NOPE)
    wvt = w[:, :, MLA_NOPE:].reshape(MLA_KV_RANK, MLA_HEADS * MLA_V).T
    return wk.astype(BF16), wvt.astype(BF16)


def _ssm_params(log_dt, a_re, a_im, b_re, b_im, c_re, c_im, n_sub):
    lr, li = a_re.astype(F32), a_im.astype(F32)
    dt = jnp.exp(log_dt.astype(F32))[:, None]
    mag = jnp.exp(lr * dt)
    abar_r = mag * jnp.cos(li * dt)
    abar_i = mag * jnp.sin(li * dt)
    den = lr * lr + li * li
    nr = abar_r - 1.0
    qr = (nr * lr + abar_i * li) / den
    qi = (abar_i * lr - nr * li) / den
    br, bi = b_re.astype(F32), b_im.astype(F32)
    bbar_r = qr[..., None] * br - qi[..., None] * bi
    bbar_i = qr[..., None] * bi + qi[..., None] * br
    pr, pi = abar_r, abar_i
    for _ in range(int(round(math.log2(n_sub)))):
        pr, pi = pr * pr - pi * pi, 2.0 * pr * pi

    hg = SSM_HALF_GROUPS
    eye = jnp.eye(hg, dtype=F32)

    def lanes(re, im):
        re = re.reshape(2, SSM_HALF_STATES)
        im = im.reshape(2, SSM_HALF_STATES)
        return jnp.concatenate([re, im], axis=1).reshape(1, 4 * SSM_HALF_STATES)

    def b_block(bb):
        bb = bb.reshape(2, hg, SSM_STATE, SSM_GROUP)
        w = jnp.einsum('hgpc,gk->hgckp', bb, eye)
        return w.reshape(2, hg * SSM_GROUP, SSM_HALF_STATES)

    def c_block(cc):
        cc = cc.reshape(2, hg, SSM_GROUP, SSM_STATE)
        w = jnp.einsum('hgcp,gk->hgpkc', cc, eye)
        return w.reshape(2, SSM_HALF_STATES, hg * SSM_GROUP)

    wb = jnp.concatenate([b_block(bbar_r), b_block(bbar_i)], axis=2).astype(BF16)
    wc = jnp.concatenate([c_block(c_re.astype(F32)), -c_block(c_im.astype(F32))], axis=1).astype(BF16)
    return wb, wc, lanes(abar_r, abar_i), lanes(pr, pi)


def _perm_matrices(t):
    n = t // SSM_STREAMS
    r = np.arange(t)
    tok = (r % SSM_STREAMS) * n + r // SSM_STREAMS
    p = np.zeros((t, t), np.float32)
    p[r, tok] = 1.0
    return jnp.asarray(p, BF16), jnp.asarray(p.T, BF16)


def kernel(x, positions, ffn1_norm, ffn1_w_gate, ffn1_w_up, ffn1_w_down, mix_norm, w_in, ssm_log_dt, ssm_a_re, ssm_a_im, ssm_b_re, ssm_b_im, ssm_c_re, ssm_c_im, ssm_d, ssm_w_glu, ssm_b_glu, swa_sinks, mla_q_norm, mla_w_uq, mla_kv_norm, mla_w_ukv, out_norm, w_out, ffn2_norm, ffn2_w_gate, ffn2_w_up, ffn2_w_down, final_norm):
    batch, seq, _ = x.shape
    depth = w_in.shape[0]
    m = batch * seq
    xf = x.reshape(m, D_MODEL)
    cos4, sin4 = _rope_tables(positions)
    t_ssm = min(SSM_T, seq)
    perm, perm_t = _perm_matrices(t_ssm)

    for l in range(depth):
        xf = _ffn(xf, ffn1_norm[:, None, :], ffn1_w_gate, ffn1_w_up, ffn1_w_down, l)
        wa, wb_q = _mla_q_weights(mla_w_uq[l])
        wk, wvt = _mla_kv_weights(mla_w_ukv[l])
        u, qs, ks, vs, qc, kc, vm = _mix_in(
            xf, mix_norm[l][None, :], _ext_in_weight(w_in[l]), cos4, sin4,
            mla_q_norm[l][None, :], mla_kv_norm[l][None, :], wa, wb_q, wk, wvt, batch, seq)
        wb, wc, lam, lamn = _ssm_params(ssm_log_dt[l], ssm_a_re[l], ssm_a_im[l], ssm_b_re[l],
                                        ssm_b_im[l], ssm_c_re[l], ssm_c_im[l], t_ssm // SSM_STREAMS)
        y_ssm = _ssm(u, perm, perm_t, wb, wc, lam, lamn, ssm_d[l][None, :],
                     ssm_w_glu[l].astype(BF16), ssm_b_glu[l][None, :], batch, seq)
        y_swa = _swa(swa_sinks[l], qs, ks, vs, batch, seq)
        y_mla = _mla(qc, kc, vm, batch, seq)
        xf = _out_proj(xf, y_ssm, y_swa, y_mla, out_norm[l][None, :], w_out[l].astype(BF16))
        xf = _ffn(xf, ffn2_norm[:, None, :], ffn2_w_gate, ffn2_w_up, ffn2_w_down, l)
    return _final_norm(xf, final_norm[None, :]).reshape(batch, seq, D_MODEL)
```

```python
import functools
import math

import jax
import jax.numpy as jnp
import numpy as np
from jax import lax
from jax.experimental import pallas as pl
from jax.experimental.pallas import tpu as pltpu

F32 = jnp.float32
BF16 = jnp.bfloat16

D_MODEL = 2048
D_FF = 5632
EPS = 1e-6
ROPE_THETA = 10000.0

SSM_GROUP = 16
SSM_WIDTH = 512
SSM_GROUPS = 32
SSM_STATE = 64
SSM_HALF_GROUPS = 16
SSM_HALF_STATES = SSM_HALF_GROUPS * SSM_STATE
SSM_STREAMS = 8

SWA_HEADS = 8
SWA_Q_PER_KV = 4
SWA_KV_HEADS = 2
SWA_HEAD_DIM = 64
SWA_WINDOW = 128
SWA_WIDTH = 512
SWA_KV_WIDTH = 128

MLA_HEADS = 8
MLA_Q_RANK = 512
MLA_KV_RANK = 256
MLA_NOPE = 128
MLA_ROPE = 64
MLA_V = 128
MLA_WIDTH = 1024
MLA_QK_PAD = 256
ROPE_DIM = 64
LANES = 128
LOG2E = 1.4426950408889634

VMEM_LIMIT = 56 * 1024 * 1024

FFN_TM = 1024
FFN_TF = 256
MIX_TM = 512
SSM_T = 512
SWA_TQ = 512
MLA_TQ = 512
MLA_TK = 512
MLA_HEADS_PER_STEP = 4
OUT_TM = 512
ROPE_TM = 2048

_C_U = 0
_C_QS = 512
_C_QS_ROT = 1024
_C_KS = 1536
_C_KS_ROT = 1664
_C_KR = 1792
_C_KR_ROT = 1920
_C_VS = 2048
_C_CQ = 2176
_C_CKV = 2688
N_EXT = 2944


def _rms(x, gain):
    ms = jnp.mean(x * x, axis=-1, keepdims=True)
    return x * lax.rsqrt(ms + EPS) * gain


def _sigmoid(x):
    return 1.0 / (1.0 + jnp.exp(-x))


def _const_spec(shape):
    nd = len(shape)
    return pl.BlockSpec(shape, lambda *_: (0,) * nd, pipeline_mode=pl.Buffered(1))


def _layer_spec(shape, layer):
    nd = len(shape)
    return pl.BlockSpec((None,) + tuple(shape), lambda *_: (layer,) + (0,) * nd,
                        pipeline_mode=pl.Buffered(1))


def _rope_kernel(pos_ref, freq_ref, sign_ref, cos_ref, sin_ref):
    ang = pos_ref[...].astype(F32) * freq_ref[...]
    cos_ref[...] = jnp.cos(ang)
    sin_ref[...] = jnp.sin(ang) * sign_ref[...]


def _rope_tables(positions):
    m = positions.size
    tm = min(ROPE_TM, m)
    inv_freq = ROPE_THETA ** (-jnp.arange(0, ROPE_DIM, 2, dtype=F32) / ROPE_DIM)
    freq4 = jnp.tile(inv_freq, 4)[None, :]
    sign4 = jnp.tile(jnp.concatenate([-jnp.ones(32, F32), jnp.ones(32, F32)]), 2)[None, :]
    return pl.pallas_call(
        _rope_kernel,
        grid=(m // tm,),
        in_specs=[pl.BlockSpec((tm, 1), lambda i: (i, 0)),
                  pl.BlockSpec((1, LANES), lambda i: (0, 0)),
                  pl.BlockSpec((1, LANES), lambda i: (0, 0))],
        out_specs=[pl.BlockSpec((tm, LANES), lambda i: (i, 0))] * 2,
        out_shape=[jax.ShapeDtypeStruct((m, LANES), F32)] * 2,
        name="rope_tables",
    )(positions.reshape(m, 1), freq4, sign4)


def _ffn_kernel(x_ref, g_ref, wg_ref, wu_ref, wd_ref, *rest, final):
    if final:
        fg_ref, o_ref, h_ref = rest
    else:
        o_ref, h_ref = rest
    f = pl.program_id(1)

    @pl.when(f == 0)
    def _():
        x = x_ref[...]
        h_ref[...] = _rms(x, g_ref[...]).astype(BF16)
        o_ref[...] = x

    h = h_ref[...]
    g = jnp.dot(h, wg_ref[...].astype(BF16), preferred_element_type=F32)
    u = jnp.dot(h, wu_ref[...].astype(BF16), preferred_element_type=F32)
    a = (g * _sigmoid(g) * (0.5 * u)).astype(BF16)
    o_ref[...] += jnp.dot(a, wd_ref[...].astype(BF16), preferred_element_type=F32)

    if final:
        @pl.when(f == pl.num_programs(1) - 1)
        def _():
            o_ref[...] = _rms(o_ref[...], fg_ref[...])


def _ffn(x, gain, wg, wu, wd, layer, final_gain=None):
    m = x.shape[0]
    tm = min(FFN_TM, m)
    final = final_gain is not None
    in_specs = [pl.BlockSpec((tm, D_MODEL), lambda i, f: (i, 0)),
                pl.BlockSpec((None, 1, D_MODEL), lambda i, f: (layer, 0, 0)),
                pl.BlockSpec((None, D_MODEL, FFN_TF), lambda i, f: (layer, 0, f)),
                pl.BlockSpec((None, D_MODEL, FFN_TF), lambda i, f: (layer, 0, f)),
                pl.BlockSpec((None, FFN_TF, D_MODEL), lambda i, f: (layer, f, 0))]
    args = [x, gain, wg, wu, wd]
    if final:
        in_specs.append(pl.BlockSpec((1, D_MODEL), lambda i, f: (0, 0)))
        args.append(final_gain)
    return pl.pallas_call(
        functools.partial(_ffn_kernel, final=final),
        grid=(m // tm, D_FF // FFN_TF),
        in_specs=in_specs,
        out_specs=pl.BlockSpec((tm, D_MODEL), lambda i, f: (i, 0)),
        out_shape=jax.ShapeDtypeStruct((m, D_MODEL), F32),
        scratch_shapes=[pltpu.VMEM((tm, D_MODEL), BF16)],
        compiler_params=pltpu.CompilerParams(
            dimension_semantics=("parallel", "arbitrary"), vmem_limit_bytes=VMEM_LIMIT),
        name="ffn",
    )(*args)


def _mix_in_kernel(x_ref, g_ref, w_ref, cos_ref, sin_ref, qn_ref, kvn_ref, wa_ref, wb_ref, wk_ref,
                   wvt_ref, u_ref, qs_ref, ks_ref, vst_ref, qc_ref, kc_ref, vt_ref):
    h = _rms(x_ref[...], g_ref[...]).astype(BF16)

    def seg(lo, n):
        return jnp.dot(h, w_ref[:, lo:lo + n], preferred_element_type=F32)

    cos4 = cos_ref[...]
    sin4 = sin_ref[...]
    cos512 = jnp.concatenate([cos4] * 4, axis=1)
    sin512 = jnp.concatenate([sin4] * 4, axis=1)

    u_ref[...] = seg(_C_U, SSM_WIDTH)
    qs = seg(_C_QS, SWA_WIDTH) * cos512 + seg(_C_QS_ROT, SWA_WIDTH) * sin512
    qs = (qs * (SWA_HEAD_DIM ** -0.5 * LOG2E)).astype(BF16)
    for hd in range(SWA_HEADS):
        qs_ref[hd] = qs[:, hd * SWA_HEAD_DIM:(hd + 1) * SWA_HEAD_DIM]
    small = seg(_C_KS, 5 * LANES)
    ks = (small[:, 0:LANES] * cos4 + small[:, LANES:2 * LANES] * sin4).astype(BF16)
    for g in range(SWA_KV_HEADS):
        ks_ref[g] = ks[:, g * SWA_HEAD_DIM:(g + 1) * SWA_HEAD_DIM]
    kr = (small[:, 2 * LANES:3 * LANES] * cos4 + small[:, 3 * LANES:4 * LANES] * sin4).astype(BF16)
    vst_ref[...] = small[:, 4 * LANES:5 * LANES].T.astype(BF16)

    cqn = _rms(seg(_C_CQ, MLA_Q_RANK), qn_ref[...]).astype(BF16)
    ckvn = _rms(seg(_C_CKV, MLA_KV_RANK), kvn_ref[...]).astype(BF16)

    scale = (MLA_NOPE + MLA_ROPE) ** -0.5 * LOG2E
    qa = jnp.dot(cqn, wa_ref[...], preferred_element_type=F32)
    qb = jnp.dot(cqn, wb_ref[...], preferred_element_type=F32)
    kn = jnp.dot(ckvn, wk_ref[...], preferred_element_type=F32)
    vt = lax.dot_general(wvt_ref[...], ckvn, (((1,), (1,)), ((), ())), preferred_element_type=F32)
    for hd in range(MLA_HEADS):
        q_nope = qa[:, hd * 256:hd * 256 + LANES]
        q_rope = qa[:, hd * 256 + LANES:(hd + 1) * 256]
        q_swap = qb[:, hd * LANES:(hd + 1) * LANES]
        qc_ref[hd, :, 0:LANES] = (q_nope * scale).astype(BF16)
        qc_ref[hd, :, LANES:2 * LANES] = ((q_rope * cos4 + q_swap * sin4) * scale).astype(BF16)
        kc_ref[hd, :, 0:LANES] = kn[:, hd * LANES:(hd + 1) * LANES].astype(BF16)
        kc_ref[hd, :, LANES:2 * LANES] = kr
        vt_ref[hd] = vt[hd * MLA_V:(hd + 1) * MLA_V, :].astype(BF16)


def _mix_in(x, gain, w_ext, cos4, sin4, qn, kvn, wa, wb, wk, wvt, layer, batch, seq):
    m = x.shape[0]
    tm = min(MIX_TM, seq)
    nt = seq // tm
    row = lambda b, i: (b * nt + i, 0)
    head_spec = lambda n, w: pl.BlockSpec((None, n, tm, w), lambda b, i: (b, 0, i, 0))
    return pl.pallas_call(
        _mix_in_kernel,
        grid=(batch, nt),
        in_specs=[pl.BlockSpec((tm, D_MODEL), row),
                  _layer_spec((1, D_MODEL), layer),
                  _layer_spec((D_MODEL, N_EXT), layer),
                  pl.BlockSpec((tm, LANES), row),
                  pl.BlockSpec((tm, LANES), row),
                  _layer_spec((1, MLA_Q_RANK), layer),
                  _layer_spec((1, MLA_KV_RANK), layer),
                  _layer_spec((MLA_Q_RANK, MLA_HEADS * 256), layer),
                  _layer_spec((MLA_Q_RANK, MLA_HEADS * LANES), layer),
                  _layer_spec((MLA_KV_RANK, MLA_HEADS * MLA_NOPE), layer),
                  _layer_spec((MLA_HEADS * MLA_V, MLA_KV_RANK), layer)],
        out_specs=[pl.BlockSpec((tm, SSM_WIDTH), row),
                   head_spec(SWA_HEADS, SWA_HEAD_DIM),
                   head_spec(SWA_KV_HEADS, SWA_HEAD_DIM),
                   pl.BlockSpec((None, SWA_KV_WIDTH, tm), lambda b, i: (b, 0, i)),
                   head_spec(MLA_HEADS, MLA_QK_PAD), head_spec(MLA_HEADS, MLA_QK_PAD),
                   pl.BlockSpec((None, MLA_HEADS, MLA_V, tm), lambda b, i: (b, 0, 0, i))],
        out_shape=[jax.ShapeDtypeStruct((m, SSM_WIDTH), F32),
                   jax.ShapeDtypeStruct((batch, SWA_HEADS, seq, SWA_HEAD_DIM), BF16),
                   jax.ShapeDtypeStruct((batch, SWA_KV_HEADS, seq, SWA_HEAD_DIM), BF16),
                   jax.ShapeDtypeStruct((batch, SWA_KV_WIDTH, seq), BF16),
                   jax.ShapeDtypeStruct((batch, MLA_HEADS, seq, MLA_QK_PAD), BF16),
                   jax.ShapeDtypeStruct((batch, MLA_HEADS, seq, MLA_QK_PAD), BF16),
                   jax.ShapeDtypeStruct((batch, MLA_HEADS, MLA_V, seq), BF16)],
        compiler_params=pltpu.CompilerParams(
            dimension_semantics=("parallel", "parallel"), vmem_limit_bytes=VMEM_LIMIT),
        name="mix_in",
    )(x, gain, w_ext, cos4, sin4, qn, kvn, wa, wb, wk, wvt)


def _ssm_kernel(u_ref, p_ref, pt_ref, wb_ref, wc_ref, lam_ref, lamn_ref, d_ref, wglu_ref, bglu_ref,
                y_ref, s_ref, carry_ref, f_ref, *, t, lane_blk):
    n = t // SSM_STREAMS
    hs = SSM_HALF_STATES

    @pl.when(pl.program_id(1) == 0)
    def _():
        carry_ref[...] = jnp.zeros_like(carry_ref)

    u = u_ref[...]
    up = jnp.dot(p_ref[...], u.astype(BF16), preferred_element_type=F32).astype(BF16)
    for hf in range(2):
        s_ref[:, hf * 2 * hs:(hf + 1) * 2 * hs] = jnp.dot(
            up[:, hf * 256:(hf + 1) * 256], wb_ref[hf], preferred_element_type=F32)

    row_id = lax.broadcasted_iota(jnp.int32, (SSM_STREAMS, lane_blk), 0)
    for hf in range(2):
        for lb in range(hs // lane_blk):
            re0 = hf * 2 * hs + lb * lane_blk
            im0 = re0 + hs
            re_sl = slice(re0, re0 + lane_blk)
            im_sl = slice(im0, im0 + lane_blk)
            ar = jnp.broadcast_to(lam_ref[:, re_sl], (SSM_STREAMS, lane_blk))
            ai = jnp.broadcast_to(lam_ref[:, im_sl], (SSM_STREAMS, lane_blk))

            def scan(x0r, x0i, store):
                def body(i, c):
                    xr, xi = c
                    r0 = pl.multiple_of(i * SSM_STREAMS, SSM_STREAMS)
                    br = s_ref[pl.ds(r0, SSM_STREAMS), re_sl]
                    bi = s_ref[pl.ds(r0, SSM_STREAMS), im_sl]
                    nr = ar * xr - ai * xi + br
                    ni = ar * xi + ai * xr + bi
                    if store:
                        s_ref[pl.ds(r0, SSM_STREAMS), re_sl] = nr
                        s_ref[pl.ds(r0, SSM_STREAMS), im_sl] = ni
                    return nr, ni
                return lax.fori_loop(0, n, body, (x0r, x0i), unroll=True)

            zero = jnp.zeros((SSM_STREAMS, lane_blk), F32)
            fr, fi = scan(zero, zero, False)
            f_ref[:, 0:lane_blk] = fr
            f_ref[:, lane_blk:2 * lane_blk] = fi
            pr = lamn_ref[:, re_sl]
            pi = lamn_ref[:, im_sl]
            sr = carry_ref[:, re_sl]
            si = carry_ref[:, im_sl]
            s0r = jnp.zeros((SSM_STREAMS, lane_blk), F32)
            s0i = jnp.zeros((SSM_STREAMS, lane_blk), F32)
            for k in range(SSM_STREAMS):
                s0r = jnp.where(row_id == k, jnp.broadcast_to(sr, s0r.shape), s0r)
                s0i = jnp.where(row_id == k, jnp.broadcast_to(si, s0i.shape), s0i)
                er = f_ref[k:k + 1, 0:lane_blk]
                ei = f_ref[k:k + 1, lane_blk:2 * lane_blk]
                sr, si = pr * sr - pi * si + er, pr * si + pi * sr + ei
            carry_ref[:, re_sl] = sr
            carry_ref[:, im_sl] = si
            scan(s0r, s0i, True)

    yp = []
    for hf in range(2):
        xb = s_ref[:, hf * 2 * hs:(hf + 1) * 2 * hs].astype(BF16)
        yp.append(jnp.dot(xb, wc_ref[hf], preferred_element_type=F32))
    yp = jnp.concatenate(yp, axis=1)
    y_hi = yp.astype(BF16)
    y_lo = (yp - y_hi.astype(F32)).astype(BF16)
    pt = pt_ref[...]
    y = (jnp.dot(pt, y_hi, preferred_element_type=F32)
         + jnp.dot(pt, y_lo, preferred_element_type=F32))
    y = y + d_ref[...] * u
    c0 = math.sqrt(2.0 / math.pi)
    y = 0.5 * y * (1.0 + jnp.tanh(c0 * (y + 0.044715 * (y * y * y))))
    z = jnp.dot(y.astype(BF16), wglu_ref[...], preferred_element_type=F32) + bglu_ref[...]
    y_ref[...] = (y * _sigmoid(z)).astype(BF16)


def _ssm(u, perm, perm_t, wb, wc, lam, lamn, d_skip, wglu, bglu, layer, batch, seq):
    m = u.shape[0]
    t = min(SSM_T, seq)
    nt = seq // t
    row = lambda b, i: (b * nt + i, 0)
    kern = functools.partial(_ssm_kernel, t=t, lane_blk=512)
    return pl.pallas_call(
        kern,
        grid=(batch, nt),
        in_specs=[pl.BlockSpec((t, SSM_WIDTH), row),
                  _const_spec((t, t)), _const_spec((t, t)),
                  _layer_spec((2, 256, 2 * SSM_HALF_STATES), layer),
                  _layer_spec((2, 2 * SSM_HALF_STATES, 256), layer),
                  _layer_spec((1, 4 * SSM_HALF_STATES), layer),
                  _layer_spec((1, 4 * SSM_HALF_STATES), layer),
                  _layer_spec((1, SSM_WIDTH), layer),
                  _layer_spec((SSM_WIDTH, SSM_WIDTH), layer),
                  _layer_spec((1, SSM_WIDTH), layer)],
        out_specs=pl.BlockSpec((t, SSM_WIDTH), row),
        out_shape=jax.ShapeDtypeStruct((m, SSM_WIDTH), BF16),
        scratch_shapes=[pltpu.VMEM((t, 4 * SSM_HALF_STATES), F32),
                        pltpu.VMEM((1, 4 * SSM_HALF_STATES), F32),
                        pltpu.VMEM((SSM_STREAMS, 1024), F32)],
        compiler_params=pltpu.CompilerParams(
            dimension_semantics=("parallel", "arbitrary"), vmem_limit_bytes=VMEM_LIMIT),
        name="ssm",
    )(u, perm, perm_t, wb, wc, lam, lamn, d_skip, wglu, bglu)


def _swa_kernel(sink_ref, q_ref, k_ref, vt_ref, o_ref, ot_ref, *, tq, layer):
    i = pl.program_id(1)
    blk = SWA_WINDOW
    nk = 2 * blk
    nq = SWA_Q_PER_KV * blk
    lane = lax.broadcasted_iota(jnp.int32, (1, nq), 1)
    key_row = lax.broadcasted_iota(jnp.int32, (nk, nq), 0)
    qry_col = lax.broadcasted_iota(jnp.int32, (nk, nq), 1) & (blk - 1)
    rel = qry_col - key_row
    for g in range(SWA_KV_HEADS):
        sink_row = jnp.zeros((1, nq), F32)
        for hh in range(SWA_Q_PER_KV):
            sink_row = jnp.where(lane // blk == hh, sink_ref[layer, g * SWA_Q_PER_KV + hh] * LOG2E, sink_row)
        for qb in range(tq // blk):
            qstart = i * tq + qb * blk
            kstart = pl.multiple_of(jnp.maximum(qstart - blk, 0), blk)
            kk = k_ref[g, pl.ds(kstart, nk), :]
            vt = vt_ref[g * SWA_HEAD_DIM:(g + 1) * SWA_HEAD_DIM, pl.ds(kstart, nk)]
            qst = q_ref[g * SWA_Q_PER_KV:(g + 1) * SWA_Q_PER_KV, qb * blk:(qb + 1) * blk, :]
            qst = qst.reshape(nq, SWA_HEAD_DIM)
            st = lax.dot_general(kk, qst, (((1,), (1,)), ((), ())), preferred_element_type=F32)
            dist = rel + (qstart - kstart)
            st = jnp.where((dist >= 0) & (dist < SWA_WINDOW), st, -jnp.inf)
            mx = jnp.maximum(jnp.max(st, axis=0, keepdims=True), sink_row)
            e = jnp.exp2(st - mx)
            den = jnp.sum(e, axis=0, keepdims=True) + jnp.exp2(sink_row - mx)
            ot = jnp.dot(vt, e.astype(BF16), preferred_element_type=F32) / den
            for hh in range(SWA_Q_PER_KV):
                hd = g * SWA_Q_PER_KV + hh
                ot_ref[hd * SWA_HEAD_DIM:(hd + 1) * SWA_HEAD_DIM, qb * blk:(qb + 1) * blk] = (
                    ot[:, hh * blk:(hh + 1) * blk])
    o_ref[...] = ot_ref[...].T.astype(BF16)


def _swa(sinks, q, k, vt, layer, batch, seq):
    tq = min(SWA_TQ, seq)
    nt = seq // tq
    kern = functools.partial(_swa_kernel, tq=tq, layer=layer)
    return pl.pallas_call(
        kern,
        grid=(batch, nt),
        in_specs=[pl.BlockSpec(memory_space=pltpu.SMEM),
                  pl.BlockSpec((None, SWA_HEADS, tq, SWA_HEAD_DIM), lambda b, i: (b, 0, i, 0)),
                  pl.BlockSpec((None, SWA_KV_HEADS, seq, SWA_HEAD_DIM), lambda b, i: (b, 0, 0, 0)),
                  pl.BlockSpec((None, SWA_KV_WIDTH, seq), lambda b, i: (b, 0, 0))],
        out_specs=pl.BlockSpec((tq, SWA_WIDTH), lambda b, i: (b * nt + i, 0)),
        out_shape=jax.ShapeDtypeStruct((batch * seq, SWA_WIDTH), BF16),
        scratch_shapes=[pltpu.VMEM((SWA_WIDTH, tq), F32)],
        compiler_params=pltpu.CompilerParams(
            dimension_semantics=("parallel", "parallel"), vmem_limit_bytes=VMEM_LIMIT),
        name="swa",
    )(sinks, q, k, vt)


def _mla_kernel(q_ref, k_ref, vt_ref, o_ref, m_ref, l_ref, acc_ref, *, tq, tk, hp):
    qi = pl.program_id(2)
    m_ref[...] = jnp.full_like(m_ref, -jnp.inf)
    l_ref[...] = jnp.zeros_like(l_ref)
    acc_ref[...] = jnp.zeros_like(acc_ref)
    sub = tq // tk

    def block(r0, diag):
        sts = []
        for hh in range(hp):
            k = k_ref[hh, pl.ds(r0, tk), :]
            st = lax.dot_general(k, q_ref[hh], (((1,), (1,)), ((), ())),
                                 preferred_element_type=F32)
            if diag is not None:
                key = lax.broadcasted_iota(jnp.int32, (tk, tq), 0) + diag * tk
                qry = lax.broadcasted_iota(jnp.int32, (tk, tq), 1)
                st = jnp.where(key <= qry, st, -jnp.inf)
            sts.append(st)
        for hh in range(hp):
            st = sts[hh]
            vt = vt_ref[hh, :, pl.ds(r0, tk)]
            m_old = m_ref[hh]
            m_new = jnp.maximum(m_old, jnp.max(st, axis=0, keepdims=True))
            alpha = jnp.exp2(m_old - m_new)
            p = jnp.exp2(st - m_new)
            l_ref[hh] = alpha * l_ref[hh] + jnp.sum(p, axis=0, keepdims=True)
            acc_ref[hh] = alpha * acc_ref[hh] + jnp.dot(vt, p.astype(BF16), preferred_element_type=F32)
            m_ref[hh] = m_new

    def body(j, c):
        block(pl.multiple_of(j * tk, tk), None)
        return c

    lax.fori_loop(0, qi * sub, body, 0)
    for d in range(sub):
        block(pl.multiple_of(qi * tq + d * tk, tk), d)
    for hh in range(hp):
        o_ref[:, hh * MLA_V:(hh + 1) * MLA_V] = (acc_ref[hh] / l_ref[hh]).T.astype(BF16)


def _mla(qc, kc, vt, batch, seq):
    tq = min(MLA_TQ, seq)
    tk = min(MLA_TK, tq)
    nt = seq // tq
    hp = MLA_HEADS_PER_STEP
    kern = functools.partial(_mla_kernel, tq=tq, tk=tk, hp=hp)
    return pl.pallas_call(
        kern,
        grid=(batch, MLA_HEADS // hp, nt),
        in_specs=[pl.BlockSpec((None, hp, tq, MLA_QK_PAD), lambda b, h, i: (b, h, i, 0)),
                  pl.BlockSpec((None, hp, seq, MLA_QK_PAD), lambda b, h, i: (b, h, 0, 0)),
                  pl.BlockSpec((None, hp, MLA_V, seq), lambda b, h, i: (b, h, 0, 0))],
        out_specs=pl.BlockSpec((tq, hp * MLA_V), lambda b, h, i: (b * nt + i, h)),
        out_shape=jax.ShapeDtypeStruct((batch * seq, MLA_WIDTH), BF16),
        scratch_shapes=[pltpu.VMEM((hp, 1, tq), F32), pltpu.VMEM((hp, 1, tq), F32),
                        pltpu.VMEM((hp, MLA_V, tq), F32)],
        compiler_params=pltpu.CompilerParams(
            dimension_semantics=("parallel", "parallel", "parallel"), vmem_limit_bytes=VMEM_LIMIT),
        name="mla",
    )(qc, kc, vt)


def _out_kernel(x_ref, ys_ref, yw_ref, ym_ref, g_ref, w_ref, o_ref):
    a = _rms(ys_ref[...].astype(F32), g_ref[:, 0:512]).astype(BF16)
    b = _rms(yw_ref[...].astype(F32), g_ref[:, 512:1024]).astype(BF16)
    c = _rms(ym_ref[...].astype(F32), g_ref[:, 1024:2048]).astype(BF16)
    acc = jnp.dot(a, w_ref[0:512, :], preferred_element_type=F32)
    acc += jnp.dot(b, w_ref[512:1024, :], preferred_element_type=F32)
    acc += jnp.dot(c, w_ref[1024:2048, :], preferred_element_type=F32)
    o_ref[...] = x_ref[...] + acc


def _out_proj(x, ys, yw, ym, gain, w, layer):
    m = x.shape[0]
    tm = min(OUT_TM, m)
    row = lambda i: (i, 0)
    return pl.pallas_call(
        _out_kernel,
        grid=(m // tm,),
        in_specs=[pl.BlockSpec((tm, D_MODEL), row),
                  pl.BlockSpec((tm, SSM_WIDTH), row),
                  pl.BlockSpec((tm, SWA_WIDTH), row),
                  pl.BlockSpec((tm, MLA_WIDTH), row),
                  _layer_spec((1, D_MODEL), layer),
                  _layer_spec((D_MODEL, D_MODEL), layer)],
        out_specs=pl.BlockSpec((tm, D_MODEL), row),
        out_shape=jax.ShapeDtypeStruct((m, D_MODEL), F32),
        compiler_params=pltpu.CompilerParams(
            dimension_semantics=("parallel",), vmem_limit_bytes=VMEM_LIMIT),
        name="out_proj",
    )(x, ys, yw, ym, gain, w)


def _swap_halves(n_heads, head_dim, base):
    half = head_dim // 2
    idx = np.arange(n_heads * head_dim).reshape(n_heads, head_dim)
    idx = np.concatenate([idx[:, half:], idx[:, :half]], axis=1).reshape(-1)
    return base + idx


def _ext_in_weight(w_in):
    o_qs, o_ks, o_vs, o_cq, o_ckv, o_kr = 512, 1024, 1152, 1280, 1792, 2048
    zeros64 = jnp.zeros((w_in.shape[0], 64), w_in.dtype)
    parts = [
        w_in[:, 0:512],
        w_in[:, o_qs:o_qs + 512],
        w_in[:, _swap_halves(SWA_HEADS, 64, o_qs)],
        w_in[:, o_ks:o_ks + 128],
        w_in[:, _swap_halves(2, 64, o_ks)],
        w_in[:, o_kr:o_kr + 64], zeros64,
        w_in[:, _swap_halves(1, 64, o_kr)], zeros64,
        w_in[:, o_vs:o_vs + 128],
        w_in[:, o_cq:o_cq + 512],
        w_in[:, o_ckv:o_ckv + 256],
    ]
    return jnp.concatenate(parts, axis=1).astype(BF16)


def _mla_q_weights(w_uq):
    w = w_uq.reshape(MLA_Q_RANK, MLA_HEADS, MLA_NOPE + MLA_ROPE)
    rope = w[:, :, MLA_NOPE:]
    z = jnp.zeros((MLA_Q_RANK, MLA_HEADS, 64), w.dtype)
    wa = jnp.concatenate([w, z], axis=2).reshape(MLA_Q_RANK, MLA_HEADS * 256)
    swapped = jnp.concatenate([rope[:, :, 32:], rope[:, :, :32]], axis=2)
    wb = jnp.concatenate([swapped, z], axis=2).reshape(MLA_Q_RANK, MLA_HEADS * LANES)
    return wa.astype(BF16), wb.astype(BF16)


def _mla_kv_weights(w_ukv):
    w = w_ukv.reshape(MLA_KV_RANK, MLA_HEADS, MLA_NOPE + MLA_V)
    wk = w[:, :, :MLA_NOPE].reshape(MLA_KV_RANK, MLA_HEADS * MLA_NOPE)
    wvt = w[:, :, MLA_NOPE:].reshape(MLA_KV_RANK, MLA_HEADS * MLA_V).T
    return wk.astype(BF16), wvt.astype(BF16)


def _ssm_params(log_dt, a_re, a_im, b_re, b_im, c_re, c_im, n_sub):
    lr, li = a_re.astype(F32), a_im.astype(F32)
    dt = jnp.exp(log_dt.astype(F32))[:, None]
    mag = jnp.exp(lr * dt)
    abar_r = mag * jnp.cos(li * dt)
    abar_i = mag * jnp.sin(li * dt)
    den = lr * lr + li * li
    nr = abar_r - 1.0
    qr = (nr * lr + abar_i * li) / den
    qi = (abar_i * lr - nr * li) / den
    br, bi = b_re.astype(F32), b_im.astype(F32)
    bbar_r = qr[..., None] * br - qi[..., None] * bi
    bbar_i = qr[..., None] * bi + qi[..., None] * br
    pr, pi = abar_r, abar_i
    for _ in range(int(round(math.log2(n_sub)))):
        pr, pi = pr * pr - pi * pi, 2.0 * pr * pi

    hg = SSM_HALF_GROUPS
    eye = jnp.eye(hg, dtype=F32)

    def lanes(re, im):
        re = re.reshape(2, SSM_HALF_STATES)
        im = im.reshape(2, SSM_HALF_STATES)
        return jnp.concatenate([re, im], axis=1).reshape(1, 4 * SSM_HALF_STATES)

    def b_block(bb):
        bb = bb.reshape(2, hg, SSM_STATE, SSM_GROUP)
        w = jnp.einsum('hgpc,gk->hgckp', bb, eye)
        return w.reshape(2, hg * SSM_GROUP, SSM_HALF_STATES)

    def c_block(cc):
        cc = cc.reshape(2, hg, SSM_GROUP, SSM_STATE)
        w = jnp.einsum('hgcp,gk->hgpkc', cc, eye)
        return w.reshape(2, SSM_HALF_STATES, hg * SSM_GROUP)

    wb = jnp.concatenate([b_block(bbar_r), b_block(bbar_i)], axis=2).astype(BF16)
    wc = jnp.concatenate([c_block(c_re.astype(F32)), -c_block(c_im.astype(F32))], axis=1).astype(BF16)
    return wb, wc, lanes(abar_r, abar_i), lanes(pr, pi)


def _perm_matrices(t):
    n = t // SSM_STREAMS
    r = np.arange(t)
    tok = (r % SSM_STREAMS) * n + r // SSM_STREAMS
    p = np.zeros((t, t), np.float32)
    p[r, tok] = 1.0
    return jnp.asarray(p, BF16), jnp.asarray(p.T, BF16)


def kernel(x, positions, ffn1_norm, ffn1_w_gate, ffn1_w_up, ffn1_w_down, mix_norm, w_in, ssm_log_dt, ssm_a_re, ssm_a_im, ssm_b_re, ssm_b_im, ssm_c_re, ssm_c_im, ssm_d, ssm_w_glu, ssm_b_glu, swa_sinks, mla_q_norm, mla_w_uq, mla_kv_norm, mla_w_ukv, out_norm, w_out, ffn2_norm, ffn2_w_gate, ffn2_w_up, ffn2_w_down, final_norm):
    batch, seq, _ = x.shape
    depth = w_in.shape[0]
    m = batch * seq
    xf = x.reshape(m, D_MODEL)
    cos4, sin4 = _rope_tables(positions)
    t_ssm = min(SSM_T, seq)
    perm, perm_t = _perm_matrices(t_ssm)

    row = lambda a: a[:, None, :]
    w_ext = jax.vmap(_ext_in_weight)(w_in)
    wa, wb_q = jax.vmap(_mla_q_weights)(mla_w_uq)
    wk, wvt = jax.vmap(_mla_kv_weights)(mla_w_ukv)
    ssm_p = functools.partial(_ssm_params, n_sub=t_ssm // SSM_STREAMS)
    wb, wc, lam, lamn = jax.vmap(ssm_p)(ssm_log_dt, ssm_a_re, ssm_a_im, ssm_b_re, ssm_b_im,
                                        ssm_c_re, ssm_c_im)
    wglu = ssm_w_glu.astype(BF16)
    w_out_b = w_out.astype(BF16)

    for l in range(depth):
        xf = _ffn(xf, row(ffn1_norm), ffn1_w_gate, ffn1_w_up, ffn1_w_down, l)
        u, qs, ks, vs, qc, kc, vm = _mix_in(
            xf, row(mix_norm), w_ext, cos4, sin4, row(mla_q_norm), row(mla_kv_norm),
            wa, wb_q, wk, wvt, l, batch, seq)
        y_ssm = _ssm(u, perm, perm_t, wb, wc, lam, lamn, row(ssm_d), wglu, row(ssm_b_glu), l, batch, seq)
        y_swa = _swa(swa_sinks, qs, ks, vs, l, batch, seq)
        y_mla = _mla(qc, kc, vm, batch, seq)
        xf = _out_proj(xf, y_ssm, y_swa, y_mla, row(out_norm), w_out_b, l)
        xf = _ffn(xf, row(ffn2_norm), ffn2_w_gate, ffn2_w_up, ffn2_w_down, l,
                  final_gain=final_norm[None, :] if l == depth - 1 else None)
    return xf.reshape(batch, seq, D_MODEL)
```

```python
import functools
import math

import jax
import jax.numpy as jnp
import numpy as np
from jax import lax
from jax.experimental import pallas as pl
from jax.experimental.pallas import tpu as pltpu

F32 = jnp.float32
BF16 = jnp.bfloat16

D_MODEL = 2048
D_FF = 5632
EPS = 1e-6
ROPE_THETA = 10000.0

SSM_GROUP = 16
SSM_WIDTH = 512
SSM_GROUPS = 32
SSM_STATE = 64
SSM_HALF_GROUPS = 16
SSM_HALF_STATES = SSM_HALF_GROUPS * SSM_STATE
SSM_STREAMS = 8

SWA_HEADS = 8
SWA_Q_PER_KV = 4
SWA_KV_HEADS = 2
SWA_HEAD_DIM = 64
SWA_WINDOW = 128
SWA_WIDTH = 512
SWA_KV_WIDTH = 128

MLA_HEADS = 8
MLA_Q_RANK = 512
MLA_KV_RANK = 256
MLA_NOPE = 128
MLA_ROPE = 64
MLA_V = 128
MLA_WIDTH = 1024
MLA_QK_PAD = 256
ROPE_DIM = 64
LANES = 128
LOG2E = 1.4426950408889634

VMEM_LIMIT = 56 * 1024 * 1024

FFN_TM = 1024
FFN_TF = 256
MIX_TM = 512
SSM_T = 512
SWA_TQ = 512
MLA_TQ = 512
MLA_TK = 512
MLA_HEADS_PER_STEP = 4
OUT_TM = 512
ROPE_TM = 2048

_C_U = 0
_C_QS = 512
_C_KS = 1024
_C_VS = 1152
_C_CQ = 1280
_C_CKV = 1792
_C_KR = 2048
N_IN = 2112
N_EXT = 2176


def _rms(x, gain):
    ms = jnp.mean(x * x, axis=-1, keepdims=True)
    return x * lax.rsqrt(ms + EPS) * gain


def _sigmoid(x):
    return 1.0 / (1.0 + jnp.exp(-x))


def _swap_halves(x):
    n = x.shape[1]
    lane = lax.broadcasted_iota(jnp.int32, x.shape, 1)
    return jnp.where((lane & 32) != 0, pltpu.roll(x, 32, 1), pltpu.roll(x, n - 32, 1))


def _rope(x, cos, sin_signed):
    return x * cos + _swap_halves(x) * sin_signed


def _const_spec(shape):
    nd = len(shape)
    return pl.BlockSpec(shape, lambda *_: (0,) * nd, pipeline_mode=pl.Buffered(1))


def _layer_spec(shape, layer):
    nd = len(shape)
    return pl.BlockSpec((None,) + tuple(shape), lambda *_: (layer,) + (0,) * nd,
                        pipeline_mode=pl.Buffered(1))


def _rope_kernel(pos_ref, freq_ref, sign_ref, cos_ref, sin_ref):
    ang = pos_ref[...].astype(F32) * freq_ref[...]
    cos_ref[...] = jnp.cos(ang)
    sin_ref[...] = jnp.sin(ang) * sign_ref[...]


def _rope_tables(positions):
    m = positions.size
    tm = min(ROPE_TM, m)
    inv_freq = ROPE_THETA ** (-jnp.arange(0, ROPE_DIM, 2, dtype=F32) / ROPE_DIM)
    freq4 = jnp.tile(inv_freq, 4)[None, :]
    sign4 = jnp.tile(jnp.concatenate([-jnp.ones(32, F32), jnp.ones(32, F32)]), 2)[None, :]
    return pl.pallas_call(
        _rope_kernel,
        grid=(m // tm,),
        in_specs=[pl.BlockSpec((tm, 1), lambda i: (i, 0)),
                  pl.BlockSpec((1, LANES), lambda i: (0, 0)),
                  pl.BlockSpec((1, LANES), lambda i: (0, 0))],
        out_specs=[pl.BlockSpec((tm, LANES), lambda i: (i, 0))] * 2,
        out_shape=[jax.ShapeDtypeStruct((m, LANES), F32)] * 2,
        name="rope_tables",
    )(positions.reshape(m, 1), freq4, sign4)


def _ffn_kernel(x_ref, g_ref, wg_ref, wu_ref, wd_ref, *rest, final):
    if final:
        fg_ref, o_ref, h_ref = rest
    else:
        o_ref, h_ref = rest
    f = pl.program_id(1)

    @pl.when(f == 0)
    def _():
        x = x_ref[...]
        h_ref[...] = _rms(x, g_ref[...]).astype(BF16)
        o_ref[...] = x

    h = h_ref[...]
    g = jnp.dot(h, wg_ref[...].astype(BF16), preferred_element_type=F32)
    u = jnp.dot(h, wu_ref[...].astype(BF16), preferred_element_type=F32)
    a = (g * _sigmoid(g) * (0.5 * u)).astype(BF16)
    o_ref[...] += jnp.dot(a, wd_ref[...].astype(BF16), preferred_element_type=F32)

    if final:
        @pl.when(f == pl.num_programs(1) - 1)
        def _():
            o_ref[...] = _rms(o_ref[...], fg_ref[...])


def _ffn(x, gain, wg, wu, wd, layer, final_gain=None):
    m = x.shape[0]
    tm = min(FFN_TM, m)
    final = final_gain is not None
    in_specs = [pl.BlockSpec((tm, D_MODEL), lambda i, f: (i, 0)),
                pl.BlockSpec((None, 1, D_MODEL), lambda i, f: (layer, 0, 0)),
                pl.BlockSpec((None, D_MODEL, FFN_TF), lambda i, f: (layer, 0, f)),
                pl.BlockSpec((None, D_MODEL, FFN_TF), lambda i, f: (layer, 0, f)),
                pl.BlockSpec((None, FFN_TF, D_MODEL), lambda i, f: (layer, f, 0))]
    args = [x, gain, wg, wu, wd]
    if final:
        in_specs.append(pl.BlockSpec((1, D_MODEL), lambda i, f: (0, 0)))
        args.append(final_gain)
    return pl.pallas_call(
        functools.partial(_ffn_kernel, final=final),
        grid=(m // tm, D_FF // FFN_TF),
        in_specs=in_specs,
        out_specs=pl.BlockSpec((tm, D_MODEL), lambda i, f: (i, 0)),
        out_shape=jax.ShapeDtypeStruct((m, D_MODEL), F32),
        scratch_shapes=[pltpu.VMEM((tm, D_MODEL), BF16)],
        compiler_params=pltpu.CompilerParams(
            dimension_semantics=("parallel", "arbitrary"), vmem_limit_bytes=VMEM_LIMIT),
        name="ffn",
    )(*args)


def _mix_in_kernel(x_ref, g_ref, w_ref, cos_ref, sin_ref, qn_ref, kvn_ref, wa_ref, wk_ref,
                   wvt_ref, u_ref, qs_ref, ks_ref, vst_ref, qc_ref, kc_ref, vt_ref):
    h = _rms(x_ref[...], g_ref[...]).astype(BF16)

    def seg(lo, n):
        return jnp.dot(h, w_ref[:, lo:lo + n], preferred_element_type=F32)

    cos4 = cos_ref[...]
    sin4 = sin_ref[...]
    cos512 = jnp.concatenate([cos4] * 4, axis=1)
    sin512 = jnp.concatenate([sin4] * 4, axis=1)

    u_ref[...] = seg(_C_U, SSM_WIDTH)
    qs = _rope(seg(_C_QS, SWA_WIDTH), cos512, sin512)
    qs = (qs * (SWA_HEAD_DIM ** -0.5 * LOG2E)).astype(BF16)
    for hd in range(SWA_HEADS):
        qs_ref[hd] = qs[:, hd * SWA_HEAD_DIM:(hd + 1) * SWA_HEAD_DIM]
    kv_s = seg(_C_KS, 2 * SWA_KV_WIDTH)
    ks = _rope(kv_s[:, 0:LANES], cos4, sin4).astype(BF16)
    for g in range(SWA_KV_HEADS):
        ks_ref[g] = ks[:, g * SWA_HEAD_DIM:(g + 1) * SWA_HEAD_DIM]
    vst_ref[...] = kv_s[:, LANES:2 * LANES].T.astype(BF16)

    cqn = _rms(seg(_C_CQ, MLA_Q_RANK), qn_ref[...]).astype(BF16)
    ckvn = _rms(seg(_C_CKV, MLA_KV_RANK), kvn_ref[...]).astype(BF16)
    kr = _rope(seg(_C_KR, LANES), cos4, sin4).astype(BF16)

    scale = (MLA_NOPE + MLA_ROPE) ** -0.5 * LOG2E
    qa = jnp.dot(cqn, wa_ref[...], preferred_element_type=F32)
    kn = jnp.dot(ckvn, wk_ref[...], preferred_element_type=F32)
    vt = lax.dot_general(wvt_ref[...], ckvn, (((1,), (1,)), ((), ())), preferred_element_type=F32)
    for hd in range(MLA_HEADS):
        q_nope = qa[:, hd * 256:hd * 256 + LANES]
        q_rope = qa[:, hd * 256 + LANES:(hd + 1) * 256]
        qc_ref[hd, :, 0:LANES] = (q_nope * scale).astype(BF16)
        qc_ref[hd, :, LANES:2 * LANES] = (_rope(q_rope, cos4, sin4) * scale).astype(BF16)
        kc_ref[hd, :, 0:LANES] = kn[:, hd * LANES:(hd + 1) * LANES].astype(BF16)
        kc_ref[hd, :, LANES:2 * LANES] = kr
        vt_ref[hd] = vt[hd * MLA_V:(hd + 1) * MLA_V, :].astype(BF16)


def _mix_in(x, gain, w_ext, cos4, sin4, qn, kvn, wa, wk, wvt, layer, batch, seq):
    m = x.shape[0]
    tm = min(MIX_TM, seq)
    nt = seq // tm
    row = lambda b, i: (b * nt + i, 0)
    head_spec = lambda n, w: pl.BlockSpec((None, n, tm, w), lambda b, i: (b, 0, i, 0))
    return pl.pallas_call(
        _mix_in_kernel,
        grid=(batch, nt),
        in_specs=[pl.BlockSpec((tm, D_MODEL), row),
                  _layer_spec((1, D_MODEL), layer),
                  _layer_spec((D_MODEL, N_EXT), layer),
                  pl.BlockSpec((tm, LANES), row),
                  pl.BlockSpec((tm, LANES), row),
                  _layer_spec((1, MLA_Q_RANK), layer),
                  _layer_spec((1, MLA_KV_RANK), layer),
                  _layer_spec((MLA_Q_RANK, MLA_HEADS * 256), layer),
                  _layer_spec((MLA_KV_RANK, MLA_HEADS * MLA_NOPE), layer),
                  _layer_spec((MLA_HEADS * MLA_V, MLA_KV_RANK), layer)],
        out_specs=[pl.BlockSpec((tm, SSM_WIDTH), row),
                   head_spec(SWA_HEADS, SWA_HEAD_DIM),
                   head_spec(SWA_KV_HEADS, SWA_HEAD_DIM),
                   pl.BlockSpec((None, SWA_KV_WIDTH, tm), lambda b, i: (b, 0, i)),
                   head_spec(MLA_HEADS, MLA_QK_PAD), head_spec(MLA_HEADS, MLA_QK_PAD),
                   pl.BlockSpec((None, MLA_HEADS, MLA_V, tm), lambda b, i: (b, 0, 0, i))],
        out_shape=[jax.ShapeDtypeStruct((m, SSM_WIDTH), F32),
                   jax.ShapeDtypeStruct((batch, SWA_HEADS, seq, SWA_HEAD_DIM), BF16),
                   jax.ShapeDtypeStruct((batch, SWA_KV_HEADS, seq, SWA_HEAD_DIM), BF16),
                   jax.ShapeDtypeStruct((batch, SWA_KV_WIDTH, seq), BF16),
                   jax.ShapeDtypeStruct((batch, MLA_HEADS, seq, MLA_QK_PAD), BF16),
                   jax.ShapeDtypeStruct((batch, MLA_HEADS, seq, MLA_QK_PAD), BF16),
                   jax.ShapeDtypeStruct((batch, MLA_HEADS, MLA_V, seq), BF16)],
        compiler_params=pltpu.CompilerParams(
            dimension_semantics=("parallel", "parallel"), vmem_limit_bytes=VMEM_LIMIT),
        name="mix_in",
    )(x, gain, w_ext, cos4, sin4, qn, kvn, wa, wk, wvt)


def _ssm_kernel(u_ref, p_ref, pt_ref, wb_ref, wc_ref, lam_ref, lamn_ref, d_ref, wglu_ref, bglu_ref,
                y_ref, s_ref, carry_ref, f_ref, *, t, lane_blk):
    n = t // SSM_STREAMS
    hs = SSM_HALF_STATES

    @pl.when(pl.program_id(1) == 0)
    def _():
        carry_ref[...] = jnp.zeros_like(carry_ref)

    u = u_ref[...]
    up = jnp.dot(p_ref[...], u.astype(BF16), preferred_element_type=F32).astype(BF16)
    for hf in range(2):
        s_ref[:, hf * 2 * hs:(hf + 1) * 2 * hs] = jnp.dot(
            up[:, hf * 256:(hf + 1) * 256], wb_ref[hf], preferred_element_type=F32)

    row_id = lax.broadcasted_iota(jnp.int32, (SSM_STREAMS, lane_blk), 0)
    for hf in range(2):
        for lb in range(hs // lane_blk):
            re0 = hf * 2 * hs + lb * lane_blk
            im0 = re0 + hs
            re_sl = slice(re0, re0 + lane_blk)
            im_sl = slice(im0, im0 + lane_blk)
            ar = jnp.broadcast_to(lam_ref[:, re_sl], (SSM_STREAMS, lane_blk))
            ai = jnp.broadcast_to(lam_ref[:, im_sl], (SSM_STREAMS, lane_blk))

            def scan(x0r, x0i, store):
                def body(i, c):
                    xr, xi = c
                    r0 = pl.multiple_of(i * SSM_STREAMS, SSM_STREAMS)
                    br = s_ref[pl.ds(r0, SSM_STREAMS), re_sl]
                    bi = s_ref[pl.ds(r0, SSM_STREAMS), im_sl]
                    nr = ar * xr - ai * xi + br
                    ni = ar * xi + ai * xr + bi
                    if store:
                        s_ref[pl.ds(r0, SSM_STREAMS), re_sl] = nr
                        s_ref[pl.ds(r0, SSM_STREAMS), im_sl] = ni
                    return nr, ni
                return lax.fori_loop(0, n, body, (x0r, x0i), unroll=True)

            zero = jnp.zeros((SSM_STREAMS, lane_blk), F32)
            fr, fi = scan(zero, zero, False)
            f_ref[:, 0:lane_blk] = fr
            f_ref[:, lane_blk:2 * lane_blk] = fi
            pr = lamn_ref[:, re_sl]
            pi = lamn_ref[:, im_sl]
            sr = carry_ref[:, re_sl]
            si = carry_ref[:, im_sl]
            s0r = jnp.zeros((SSM_STREAMS, lane_blk), F32)
            s0i = jnp.zeros((SSM_STREAMS, lane_blk), F32)
            for k in range(SSM_STREAMS):
                s0r = jnp.where(row_id == k, jnp.broadcast_to(sr, s0r.shape), s0r)
                s0i = jnp.where(row_id == k, jnp.broadcast_to(si, s0i.shape), s0i)
                er = f_ref[k:k + 1, 0:lane_blk]
                ei = f_ref[k:k + 1, lane_blk:2 * lane_blk]
                sr, si = pr * sr - pi * si + er, pr * si + pi * sr + ei
            carry_ref[:, re_sl] = sr
            carry_ref[:, im_sl] = si
            scan(s0r, s0i, True)

    yp = []
    for hf in range(2):
        xb = s_ref[:, hf * 2 * hs:(hf + 1) * 2 * hs].astype(BF16)
        yp.append(jnp.dot(xb, wc_ref[hf], preferred_element_type=F32))
    yp = jnp.concatenate(yp, axis=1)
    y_hi = yp.astype(BF16)
    y_lo = (yp - y_hi.astype(F32)).astype(BF16)
    pt = pt_ref[...]
    y = (jnp.dot(pt, y_hi, preferred_element_type=F32)
         + jnp.dot(pt, y_lo, preferred_element_type=F32))
    y = y + d_ref[...] * u
    c0 = math.sqrt(2.0 / math.pi)
    y = 0.5 * y * (1.0 + jnp.tanh(c0 * (y + 0.044715 * (y * y * y))))
    z = jnp.dot(y.astype(BF16), wglu_ref[...], preferred_element_type=F32) + bglu_ref[...]
    y_ref[...] = (y * _sigmoid(z)).astype(BF16)


def _ssm(u, perm, perm_t, wb, wc, lam, lamn, d_skip, wglu, bglu, layer, batch, seq):
    m = u.shape[0]
    t = min(SSM_T, seq)
    nt = seq // t
    row = lambda b, i: (b * nt + i, 0)
    kern = functools.partial(_ssm_kernel, t=t, lane_blk=512)
    return pl.pallas_call(
        kern,
        grid=(batch, nt),
        in_specs=[pl.BlockSpec((t, SSM_WIDTH), row),
                  _const_spec((t, t)), _const_spec((t, t)),
                  _layer_spec((2, 256, 2 * SSM_HALF_STATES), layer),
                  _layer_spec((2, 2 * SSM_HALF_STATES, 256), layer),
                  _layer_spec((1, 4 * SSM_HALF_STATES), layer),
                  _layer_spec((1, 4 * SSM_HALF_STATES), layer),
                  _layer_spec((1, SSM_WIDTH), layer),
                  _layer_spec((SSM_WIDTH, SSM_WIDTH), layer),
                  _layer_spec((1, SSM_WIDTH), layer)],
        out_specs=pl.BlockSpec((t, SSM_WIDTH), row),
        out_shape=jax.ShapeDtypeStruct((m, SSM_WIDTH), BF16),
        scratch_shapes=[pltpu.VMEM((t, 4 * SSM_HALF_STATES), F32),
                        pltpu.VMEM((1, 4 * SSM_HALF_STATES), F32),
                        pltpu.VMEM((SSM_STREAMS, 1024), F32)],
        compiler_params=pltpu.CompilerParams(
            dimension_semantics=("parallel", "arbitrary"), vmem_limit_bytes=VMEM_LIMIT),
        name="ssm",
    )(u, perm, perm_t, wb, wc, lam, lamn, d_skip, wglu, bglu)


def _swa_kernel(sink_ref, q_ref, k_ref, vt_ref, o_ref, ot_ref, *, tq, layer):
    i = pl.program_id(1)
    blk = SWA_WINDOW
    nk = 2 * blk
    nq = SWA_Q_PER_KV * blk
    lane = lax.broadcasted_iota(jnp.int32, (1, nq), 1)
    key_row = lax.broadcasted_iota(jnp.int32, (nk, nq), 0)
    qry_col = lax.broadcasted_iota(jnp.int32, (nk, nq), 1) & (blk - 1)
    rel = qry_col - key_row
    for g in range(SWA_KV_HEADS):
        sink_row = jnp.zeros((1, nq), F32)
        for hh in range(SWA_Q_PER_KV):
            sink_row = jnp.where(lane // blk == hh, sink_ref[layer, g * SWA_Q_PER_KV + hh] * LOG2E, sink_row)
        for qb in range(tq // blk):
            qstart = i * tq + qb * blk
            kstart = pl.multiple_of(jnp.maximum(qstart - blk, 0), blk)
            kk = k_ref[g, pl.ds(kstart, nk), :]
            vt = vt_ref[g * SWA_HEAD_DIM:(g + 1) * SWA_HEAD_DIM, pl.ds(kstart, nk)]
            qst = q_ref[g * SWA_Q_PER_KV:(g + 1) * SWA_Q_PER_KV, qb * blk:(qb + 1) * blk, :]
            qst = qst.reshape(nq, SWA_HEAD_DIM)
            st = lax.dot_general(kk, qst, (((1,), (1,)), ((), ())), preferred_element_type=F32)
            dist = rel + (qstart - kstart)
            st = jnp.where((dist >= 0) & (dist < SWA_WINDOW), st, -jnp.inf)
            mx = jnp.maximum(jnp.max(st, axis=0, keepdims=True), sink_row)
            e = jnp.exp2(st - mx)
            den = jnp.sum(e, axis=0, keepdims=True) + jnp.exp2(sink_row - mx)
            ot = jnp.dot(vt, e.astype(BF16), preferred_element_type=F32) / den
            for hh in range(SWA_Q_PER_KV):
                hd = g * SWA_Q_PER_KV + hh
                ot_ref[hd * SWA_HEAD_DIM:(hd + 1) * SWA_HEAD_DIM, qb * blk:(qb + 1) * blk] = (
                    ot[:, hh * blk:(hh + 1) * blk])
    o_ref[...] = ot_ref[...].T.astype(BF16)


def _swa(sinks, q, k, vt, layer, batch, seq):
    tq = min(SWA_TQ, seq)
    nt = seq // tq
    kern = functools.partial(_swa_kernel, tq=tq, layer=layer)
    return pl.pallas_call(
        kern,
        grid=(batch, nt),
        in_specs=[pl.BlockSpec(memory_space=pltpu.SMEM),
                  pl.BlockSpec((None, SWA_HEADS, tq, SWA_HEAD_DIM), lambda b, i: (b, 0, i, 0)),
                  pl.BlockSpec((None, SWA_KV_HEADS, seq, SWA_HEAD_DIM), lambda b, i: (b, 0, 0, 0)),
                  pl.BlockSpec((None, SWA_KV_WIDTH, seq), lambda b, i: (b, 0, 0))],
        out_specs=pl.BlockSpec((tq, SWA_WIDTH), lambda b, i: (b * nt + i, 0)),
        out_shape=jax.ShapeDtypeStruct((batch * seq, SWA_WIDTH), BF16),
        scratch_shapes=[pltpu.VMEM((SWA_WIDTH, tq), F32)],
        compiler_params=pltpu.CompilerParams(
            dimension_semantics=("parallel", "parallel"), vmem_limit_bytes=VMEM_LIMIT),
        name="swa",
    )(sinks, q, k, vt)


def _mla_kernel(q_ref, k_ref, vt_ref, o_ref, m_ref, l_ref, acc_ref, *, tq, tk, hp):
    qi = pl.program_id(2)
    m_ref[...] = jnp.full_like(m_ref, -jnp.inf)
    l_ref[...] = jnp.zeros_like(l_ref)
    acc_ref[...] = jnp.zeros_like(acc_ref)
    sub = tq // tk

    def block(r0, diag):
        sts = []
        for hh in range(hp):
            k = k_ref[hh, pl.ds(r0, tk), :]
            st = lax.dot_general(k, q_ref[hh], (((1,), (1,)), ((), ())),
                                 preferred_element_type=F32)
            if diag is not None:
                key = lax.broadcasted_iota(jnp.int32, (tk, tq), 0) + diag * tk
                qry = lax.broadcasted_iota(jnp.int32, (tk, tq), 1)
                st = jnp.where(key <= qry, st, -jnp.inf)
            sts.append(st)
        for hh in range(hp):
            st = sts[hh]
            vt = vt_ref[hh, :, pl.ds(r0, tk)]
            m_old = m_ref[hh]
            m_new = jnp.maximum(m_old, jnp.max(st, axis=0, keepdims=True))
            alpha = jnp.exp2(m_old - m_new)
            p = jnp.exp2(st - m_new)
            l_ref[hh] = alpha * l_ref[hh] + jnp.sum(p, axis=0, keepdims=True)
            acc_ref[hh] = alpha * acc_ref[hh] + jnp.dot(vt, p.astype(BF16), preferred_element_type=F32)
            m_ref[hh] = m_new

    def body(j, c):
        block(pl.multiple_of(j * tk, tk), None)
        return c

    lax.fori_loop(0, qi * sub, body, 0)
    for d in range(sub):
        block(pl.multiple_of(qi * tq + d * tk, tk), d)
    for hh in range(hp):
        o_ref[:, hh * MLA_V:(hh + 1) * MLA_V] = (acc_ref[hh] / l_ref[hh]).T.astype(BF16)


def _mla(qc, kc, vt, batch, seq):
    tq = min(MLA_TQ, seq)
    tk = min(MLA_TK, tq)
    nt = seq // tq
    hp = MLA_HEADS_PER_STEP
    kern = functools.partial(_mla_kernel, tq=tq, tk=tk, hp=hp)
    return pl.pallas_call(
        kern,
        grid=(batch, MLA_HEADS // hp, nt),
        in_specs=[pl.BlockSpec((None, hp, tq, MLA_QK_PAD), lambda b, h, i: (b, h, i, 0)),
                  pl.BlockSpec((None, hp, seq, MLA_QK_PAD), lambda b, h, i: (b, h, 0, 0)),
                  pl.BlockSpec((None, hp, MLA_V, seq), lambda b, h, i: (b, h, 0, 0))],
        out_specs=pl.BlockSpec((tq, hp * MLA_V), lambda b, h, i: (b * nt + i, h)),
        out_shape=jax.ShapeDtypeStruct((batch * seq, MLA_WIDTH), BF16),
        scratch_shapes=[pltpu.VMEM((hp, 1, tq), F32), pltpu.VMEM((hp, 1, tq), F32),
                        pltpu.VMEM((hp, MLA_V, tq), F32)],
        compiler_params=pltpu.CompilerParams(
            dimension_semantics=("parallel", "parallel", "parallel"), vmem_limit_bytes=VMEM_LIMIT),
        name="mla",
    )(qc, kc, vt)


def _out_kernel(x_ref, ys_ref, yw_ref, ym_ref, g_ref, w_ref, o_ref):
    a = _rms(ys_ref[...].astype(F32), g_ref[:, 0:512]).astype(BF16)
    b = _rms(yw_ref[...].astype(F32), g_ref[:, 512:1024]).astype(BF16)
    c = _rms(ym_ref[...].astype(F32), g_ref[:, 1024:2048]).astype(BF16)
    acc = jnp.dot(a, w_ref[0:512, :], preferred_element_type=F32)
    acc += jnp.dot(b, w_ref[512:1024, :], preferred_element_type=F32)
    acc += jnp.dot(c, w_ref[1024:2048, :], preferred_element_type=F32)
    o_ref[...] = x_ref[...] + acc


def _out_proj(x, ys, yw, ym, gain, w, layer):
    m = x.shape[0]
    tm = min(OUT_TM, m)
    row = lambda i: (i, 0)
    return pl.pallas_call(
        _out_kernel,
        grid=(m // tm,),
        in_specs=[pl.BlockSpec((tm, D_MODEL), row),
                  pl.BlockSpec((tm, SSM_WIDTH), row),
                  pl.BlockSpec((tm, SWA_WIDTH), row),
                  pl.BlockSpec((tm, MLA_WIDTH), row),
                  _layer_spec((1, D_MODEL), layer),
                  _layer_spec((D_MODEL, D_MODEL), layer)],
        out_specs=pl.BlockSpec((tm, D_MODEL), row),
        out_shape=jax.ShapeDtypeStruct((m, D_MODEL), F32),
        compiler_params=pltpu.CompilerParams(
            dimension_semantics=("parallel",), vmem_limit_bytes=VMEM_LIMIT),
        name="out_proj",
    )(x, ys, yw, ym, gain, w)


def _ext_in_weight(w_in):
    return jnp.pad(w_in, ((0, 0), (0, N_EXT - N_IN))).astype(BF16)


def _mla_q_weight(w_uq):
    w = w_uq.reshape(MLA_Q_RANK, MLA_HEADS, MLA_NOPE + MLA_ROPE)
    w = jnp.pad(w, ((0, 0), (0, 0), (0, MLA_QK_PAD - MLA_NOPE - MLA_ROPE)))
    return w.reshape(MLA_Q_RANK, MLA_HEADS * MLA_QK_PAD).astype(BF16)


def _mla_kv_weights(w_ukv):
    w = w_ukv.reshape(MLA_KV_RANK, MLA_HEADS, MLA_NOPE + MLA_V)
    wk = w[:, :, :MLA_NOPE].reshape(MLA_KV_RANK, MLA_HEADS * MLA_NOPE)
    wvt = w[:, :, MLA_NOPE:].reshape(MLA_KV_RANK, MLA_HEADS * MLA_V).T
    return wk.astype(BF16), wvt.astype(BF16)


def _ssm_params(log_dt, a_re, a_im, b_re, b_im, c_re, c_im, n_sub):
    lr, li = a_re.astype(F32), a_im.astype(F32)
    dt = jnp.exp(log_dt.astype(F32))[:, None]
    mag = jnp.exp(lr * dt)
    abar_r = mag * jnp.cos(li * dt)
    abar_i = mag * jnp.sin(li * dt)
    den = lr * lr + li * li
    nr = abar_r - 1.0
    qr = (nr * lr + abar_i * li) / den
    qi = (abar_i * lr - nr * li) / den
    br, bi = b_re.astype(F32), b_im.astype(F32)
    bbar_r = qr[..., None] * br - qi[..., None] * bi
    bbar_i = qr[..., None] * bi + qi[..., None] * br
    pr, pi = abar_r, abar_i
    for _ in range(int(round(math.log2(n_sub)))):
        pr, pi = pr * pr - pi * pi, 2.0 * pr * pi

    hg = SSM_HALF_GROUPS
    eye = jnp.eye(hg, dtype=F32)

    def lanes(re, im):
        re = re.reshape(2, SSM_HALF_STATES)
        im = im.reshape(2, SSM_HALF_STATES)
        return jnp.concatenate([re, im], axis=1).reshape(1, 4 * SSM_HALF_STATES)

    def b_block(bb):
        bb = bb.reshape(2, hg, SSM_STATE, SSM_GROUP)
        w = jnp.einsum('hgpc,gk->hgckp', bb, eye)
        return w.reshape(2, hg * SSM_GROUP, SSM_HALF_STATES)

    def c_block(cc):
        cc = cc.reshape(2, hg, SSM_GROUP, SSM_STATE)
        w = jnp.einsum('hgcp,gk->hgpkc', cc, eye)
        return w.reshape(2, SSM_HALF_STATES, hg * SSM_GROUP)

    wb = jnp.concatenate([b_block(bbar_r), b_block(bbar_i)], axis=2).astype(BF16)
    wc = jnp.concatenate([c_block(c_re.astype(F32)), -c_block(c_im.astype(F32))], axis=1).astype(BF16)
    return wb, wc, lanes(abar_r, abar_i), lanes(pr, pi)


def _perm_matrices(t):
    n = t // SSM_STREAMS
    r = np.arange(t)
    tok = (r % SSM_STREAMS) * n + r // SSM_STREAMS
    p = np.zeros((t, t), np.float32)
    p[r, tok] = 1.0
    return jnp.asarray(p, BF16), jnp.asarray(p.T, BF16)


def kernel(x, positions, ffn1_norm, ffn1_w_gate, ffn1_w_up, ffn1_w_down, mix_norm, w_in, ssm_log_dt, ssm_a_re, ssm_a_im, ssm_b_re, ssm_b_im, ssm_c_re, ssm_c_im, ssm_d, ssm_w_glu, ssm_b_glu, swa_sinks, mla_q_norm, mla_w_uq, mla_kv_norm, mla_w_ukv, out_norm, w_out, ffn2_norm, ffn2_w_gate, ffn2_w_up, ffn2_w_down, final_norm):
    batch, seq, _ = x.shape
    depth = w_in.shape[0]
    m = batch * seq
    xf = x.reshape(m, D_MODEL)
    cos4, sin4 = _rope_tables(positions)
    t_ssm = min(SSM_T, seq)
    perm, perm_t = _perm_matrices(t_ssm)

    row = lambda a: a[:, None, :]
    w_ext = jax.vmap(_ext_in_weight)(w_in)
    wa = jax.vmap(_mla_q_weight)(mla_w_uq)
    wk, wvt = jax.vmap(_mla_kv_weights)(mla_w_ukv)
    ssm_p = functools.partial(_ssm_params, n_sub=t_ssm // SSM_STREAMS)
    wb, wc, lam, lamn = jax.vmap(ssm_p)(ssm_log_dt, ssm_a_re, ssm_a_im, ssm_b_re, ssm_b_im,
                                        ssm_c_re, ssm_c_im)
    wglu = ssm_w_glu.astype(BF16)
    w_out_b = w_out.astype(BF16)

    for l in range(depth):
        xf = _ffn(xf, row(ffn1_norm), ffn1_w_gate, ffn1_w_up, ffn1_w_down, l)
        u, qs, ks, vs, qc, kc, vm = _mix_in(
            xf, row(mix_norm), w_ext, cos4, sin4, row(mla_q_norm), row(mla_kv_norm),
            wa, wk, wvt, l, batch, seq)
        y_ssm = _ssm(u, perm, perm_t, wb, wc, lam, lamn, row(ssm_d), wglu, row(ssm_b_glu), l, batch, seq)
        y_swa = _swa(swa_sinks, qs, ks, vs, l, batch, seq)
        y_mla = _mla(qc, kc, vm, batch, seq)
        xf = _out_proj(xf, y_ssm, y_swa, y_mla, row(out_norm), w_out_b, l)
        xf = _ffn(xf, row(ffn2_norm), ffn2_w_gate, ffn2_w_up, ffn2_w_down, l,
                  final_gain=final_norm[None, :] if l == depth - 1 else None)
    return xf.reshape(batch, seq, D_MODEL)
```
